```python
import math
import jax, jax.numpy as jnp
from jax import lax
import numpy as np

D_MODEL = 4096
BATCH = 1
SEQ = 8192
DEPTH = 1

PLE_DIM = 256
N_ATT_HEADS = 8
ATT_HEAD_DIM = 128
ATT_V_DIM = 2 * ATT_HEAD_DIM
ATT_WIDTH = N_ATT_HEADS * ATT_V_DIM
Q_BLOCK = 128
CONV_WIDTH = D_MODEL // 2
CONV_KERNEL = 31
Q_COLS = N_ATT_HEADS * 2 * ATT_HEAD_DIM
K_COLS = N_ATT_HEADS * 2 * ATT_HEAD_DIM
V_COLS = N_ATT_HEADS * ATT_V_DIM
ZA_COLS = ATT_WIDTH
GLU_COLS = 2 * CONV_WIDTH
ZC_COLS = CONV_WIDTH
GATE_COLS = 2 * D_MODEL
_SIZES = [Q_COLS, K_COLS, V_COLS, ZA_COLS, GLU_COLS, ZC_COLS, D_MODEL, D_MODEL]
IN_COLS = sum(_SIZES)
SPLIT_POINTS = [int(v) for v in np.cumsum(_SIZES)[:-1]]

RMS_EPS = 1e-6
SUBLN_EPS = 1e-5
LN_EPS = 1e-5

kernel_name = 'hybrid_diffattn_conformer_gated_block'


def rms_norm(x, g, eps=RMS_EPS):
    xf = x.astype(jnp.float32)
    y = xf * lax.rsqrt(jnp.mean(xf * xf, axis=-1, keepdims=True) + eps)
    return (y * g.astype(jnp.float32)).astype(x.dtype)


def layer_norm(x, g, b, eps=LN_EPS):
    xf = x.astype(jnp.float32)
    mu = jnp.mean(xf, axis=-1, keepdims=True)
    xc = xf - mu
    y = xc * lax.rsqrt(jnp.mean(xc * xc, axis=-1, keepdims=True) + eps)
    return (y * g.astype(jnp.float32) + b.astype(jnp.float32)).astype(x.dtype)


def diff_attention(q, k, v, lam):
    b, s = q.shape[0], q.shape[1]
    nb = s // Q_BLOCK
    qb = q.reshape(b, nb, Q_BLOCK, N_ATT_HEADS, 2, ATT_HEAD_DIM).transpose(1, 0, 2, 3, 4, 5)
    scale = ATT_HEAD_DIM ** -0.5
    k_pos = jnp.arange(s)

    def block(args):
        q_blk, i = args
        sc = jnp.einsum('bqhcd,bkhcd->bhcqk', q_blk, k, preferred_element_type=jnp.float32) * scale
        q_pos = i * Q_BLOCK + jnp.arange(Q_BLOCK)
        causal = k_pos[None, :] <= q_pos[:, None]
        sc = jnp.where(causal, sc, -jnp.inf)
        pr = jax.nn.softmax(sc, axis=-1)
        att = pr[:, :, 0] - lam * pr[:, :, 1]
        return jnp.einsum('bhqk,bkhd->bqhd', att.astype(v.dtype), v)

    out = lax.map(block, (qb, jnp.arange(nb)))
    return out.transpose(1, 0, 2, 3, 4).reshape(b, s, N_ATT_HEADS, ATT_V_DIM)


def causal_depthwise_conv(u, w, bias):
    y = lax.conv_general_dilated(
        u, w[:, None, :].astype(u.dtype), window_strides=(1,),
        padding=[(CONV_KERNEL - 1, 0)],
        dimension_numbers=('NWC', 'WIO', 'NWC'),
        feature_group_count=u.shape[-1])
    return y + bias


def setup_inputs(seed: int = 0) -> dict:
    key = jax.random.key(seed)
    ks = jax.random.split(key, 24)
    f32 = jnp.float32

    def nrm(k, shape, std):
        return jax.random.normal(k, shape, f32) * std

    def gain(k, shape):
        return 1.0 + 0.02 * jax.random.normal(k, shape, f32)

    return {
        'x': jax.random.normal(ks[0], (BATCH, SEQ, D_MODEL), f32),
        'p': jax.random.normal(ks[1], (DEPTH, BATCH, SEQ, PLE_DIM), f32),
        'g_mix': gain(ks[2], (DEPTH, D_MODEL)),
        'w_in': nrm(ks[3], (DEPTH, D_MODEL, IN_COLS), D_MODEL ** -0.5),
        'lambda_q1': nrm(ks[4], (DEPTH, ATT_HEAD_DIM), 0.1),
        'lambda_k1': nrm(ks[5], (DEPTH, ATT_HEAD_DIM), 0.1),
        'lambda_q2': nrm(ks[6], (DEPTH, ATT_HEAD_DIM), 0.1),
        'lambda_k2': nrm(ks[7], (DEPTH, ATT_HEAD_DIM), 0.1),
        'g_subln': gain(ks[8], (DEPTH, ATT_V_DIM)),
        'w_att_out': nrm(ks[9], (DEPTH, ATT_WIDTH, D_MODEL), ATT_WIDTH ** -0.5),
        'conv_w': nrm(ks[10], (DEPTH, CONV_KERNEL, CONV_WIDTH), CONV_KERNEL ** -0.5),
        'conv_b': nrm(ks[11], (DEPTH, CONV_WIDTH), 0.02),
        'ln_g': gain(ks[12], (DEPTH, CONV_WIDTH)),
        'ln_b': nrm(ks[13], (DEPTH, CONV_WIDTH), 0.02),
        'w_conv_out': nrm(ks[14], (DEPTH, CONV_WIDTH, D_MODEL), CONV_WIDTH ** -0.5),
        'w_out': nrm(ks[15], (DEPTH, D_MODEL, D_MODEL), D_MODEL ** -0.5),
        'g_ple_gate': gain(ks[16], (DEPTH, D_MODEL)),
        'w_ple_gate': nrm(ks[17], (DEPTH, D_MODEL, D_MODEL), D_MODEL ** -0.5),
        'w_ple': nrm(ks[18], (DEPTH, PLE_DIM, D_MODEL), PLE_DIM ** -0.5),
        'g_ple_post': gain(ks[19], (DEPTH, D_MODEL)),
        'g_final': gain(ks[20], (D_MODEL,)),
    }


def reference(x, p, g_mix, w_in, lambda_q1, lambda_k1, lambda_q2, lambda_k2, g_subln,
              w_att_out, conv_w, conv_b, ln_g, ln_b, w_conv_out, w_out,
              g_ple_gate, w_ple_gate, w_ple, g_ple_post, g_final):
    b, s, _ = x.shape
    for i in range(DEPTH):
        lambda_init = 0.8 - 0.6 * math.exp(-0.3 * i)
        h = rms_norm(x, g_mix[i])
        proj = h @ w_in[i]
        q, k, v, z_a, glu, z_c, gate_a, gate_c = jnp.split(proj, SPLIT_POINTS, axis=-1)

        q = q.reshape(b, s, N_ATT_HEADS, 2, ATT_HEAD_DIM)
        k = k.reshape(b, s, N_ATT_HEADS, 2, ATT_HEAD_DIM)
        v = v.reshape(b, s, N_ATT_HEADS, ATT_V_DIM)
        lam = (jnp.exp(jnp.sum(lambda_q1[i].astype(jnp.float32) * lambda_k1[i].astype(jnp.float32)))
               - jnp.exp(jnp.sum(lambda_q2[i].astype(jnp.float32) * lambda_k2[i].astype(jnp.float32)))
               + lambda_init)
        o = diff_attention(q, k, v, lam)
        o = rms_norm(o, g_subln[i], SUBLN_EPS) * (1.0 - lambda_init)
        y_a = (o.reshape(b, s, ATT_WIDTH) * jax.nn.silu(z_a)) @ w_att_out[i]

        ga, gb = jnp.split(glu, 2, axis=-1)
        u = ga * jax.nn.sigmoid(gb)
        u = causal_depthwise_conv(u, conv_w[i], conv_b[i])
        u = jax.nn.silu(layer_norm(u, ln_g[i], ln_b[i]))
        y_c = (u * jax.nn.silu(z_c)) @ w_conv_out[i]

        m = jax.nn.sigmoid(gate_a) * y_a + jax.nn.sigmoid(gate_c) * y_c
        x = x + m @ w_out[i]

        e = rms_norm(p[i] @ w_ple[i], g_ple_post[i])
        g = jax.nn.sigmoid(rms_norm(x, g_ple_gate[i]) @ w_ple_gate[i])
        x = x + g * e
    return rms_norm(x, g_final)
```

```python
import functools
import math

import jax
import jax.numpy as jnp
from jax import lax
from jax.experimental import pallas as pl
from jax.experimental.pallas import tpu as pltpu

D_MODEL = 4096
SEQ = 8192
PLE_DIM = 256
N_HEADS = 8
HEAD_DIM = 128
V_DIM = 2 * HEAD_DIM
ATT_WIDTH = N_HEADS * V_DIM
CONV_WIDTH = D_MODEL // 2
CONV_KERNEL = 31
QKV_COLS = 3 * ATT_WIDTH
REST_COLS = ATT_WIDTH + 3 * CONV_WIDTH + 2 * D_MODEL

RMS_EPS = 1e-6
SUBLN_EPS = 1e-5
LN_EPS = 1e-5
LAMBDA_INIT = 0.8 - 0.6 * math.exp(-0.3 * 0)

VMEM_LIMIT_BYTES = 56 * 1024 * 1024
CONV_HALO = 32

BF16 = jnp.bfloat16
F32 = jnp.float32


def _params(*semantics):
    return pltpu.CompilerParams(dimension_semantics=semantics, vmem_limit_bytes=VMEM_LIMIT_BYTES)


def _rms_scale(v, eps):
    return lax.rsqrt(jnp.mean(v * v, axis=-1, keepdims=True) + eps)


def _silu(v):
    return v * jax.nn.sigmoid(v)


def _rmsnorm_kernel(x_ref, g_ref, o_ref):
    x = x_ref[...]
    o_ref[...] = (x * _rms_scale(x, RMS_EPS) * g_ref[...]).astype(o_ref.dtype)


def _rmsnorm(x, g, tm=512):
    s, d = x.shape
    return pl.pallas_call(
        _rmsnorm_kernel,
        grid=(s // tm,),
        in_specs=[pl.BlockSpec((tm, d), lambda i: (i, 0)),
                  pl.BlockSpec((1, d), lambda i: (0, 0))],
        out_specs=pl.BlockSpec((tm, d), lambda i: (i, 0)),
        out_shape=jax.ShapeDtypeStruct((s, d), BF16),
        compiler_params=_params("parallel"),
        name="rmsnorm_in",
    )(x, g.reshape(1, d))


def _matmul_kernel(a_ref, w_ref, o_ref):
    o_ref[...] = jnp.dot(a_ref[...], w_ref[...], preferred_element_type=F32).astype(o_ref.dtype)


def _matmul(a, w, col_block_offset, n_cols, out_dtype, tm, tn, name):
    s, k = a.shape
    return pl.pallas_call(
        _matmul_kernel,
        grid=(s // tm, n_cols // tn),
        in_specs=[pl.BlockSpec((tm, k), lambda i, j: (i, 0)),
                  pl.BlockSpec((k, tn), lambda i, j: (0, j + col_block_offset))],
        out_specs=pl.BlockSpec((tm, tn), lambda i, j: (i, j)),
        out_shape=jax.ShapeDtypeStruct((s, n_cols), out_dtype),
        compiler_params=_params("parallel", "arbitrary"),
        name=name,
    )(a, w)


def _attn_kernel(lq1_ref, lk1_ref, lq2_ref, lk2_ref, gsub_ref, q_ref, k_ref, v_ref, za_ref,
                 o_ref, acc1_ref, acc2_ref, *, t):
    qi = pl.program_id(1)
    scale = HEAD_DIM ** -0.5
    lam = (jnp.exp(jnp.sum(lq1_ref[...] * lk1_ref[...], axis=-1, keepdims=True))
           - jnp.exp(jnp.sum(lq2_ref[...] * lk2_ref[...], axis=-1, keepdims=True))
           + LAMBDA_INIT)

    q = q_ref[...]
    q1 = q[:, :HEAD_DIM]
    q2 = q[:, HEAD_DIM:]
    acc1_ref[...] = jnp.zeros_like(acc1_ref)
    acc2_ref[...] = jnp.zeros_like(acc2_ref)

    def scores(qc, kc, masked):
        s = lax.dot_general(qc, kc, (((1,), (1,)), ((), ())), preferred_element_type=F32) * scale
        if masked:
            row = lax.broadcasted_iota(jnp.int32, (t, t), 0)
            col = lax.broadcasted_iota(jnp.int32, (t, t), 1)
            s = jnp.where(col <= row, s, -jnp.inf)
        return s

    def online(s, m, l, acc_ref, vj):
        m_new = jnp.maximum(m, jnp.max(s, axis=-1, keepdims=True))
        alpha = jnp.exp(m - m_new)
        p = jnp.exp(s - m_new)
        l_new = alpha * l + jnp.sum(p, axis=-1, keepdims=True)
        acc_ref[...] = alpha * acc_ref[...] + jnp.dot(p.astype(BF16), vj, preferred_element_type=F32)
        return m_new, l_new

    def step(j, carry, masked):
        m1, l1, m2, l2 = carry
        r0 = pl.multiple_of(j * t, t)
        kj = k_ref[pl.ds(r0, t), :]
        vj = v_ref[pl.ds(r0, t), :]
        m1, l1 = online(scores(q1, kj[:, :HEAD_DIM], masked), m1, l1, acc1_ref, vj)
        m2, l2 = online(scores(q2, kj[:, HEAD_DIM:], masked), m2, l2, acc2_ref, vj)
        return m1, l1, m2, l2

    neg = jnp.full((t, 1), -jnp.inf, F32)
    zero = jnp.zeros((t, 1), F32)
    carry = lax.fori_loop(0, qi, functools.partial(step, masked=False), (neg, zero, neg, zero))
    _, l1, _, l2 = step(qi, carry, masked=True)

    o = acc1_ref[...] / l1 - lam * (acc2_ref[...] / l2)
    o = o * _rms_scale(o, SUBLN_EPS) * gsub_ref[...] * (1.0 - LAMBDA_INIT)
    o_ref[...] = (o * _silu(za_ref[...])).astype(o_ref.dtype)


def _attention(qkv, rest, lq1, lk1, lq2, lk2, g_subln, t=512):
    s = qkv.shape[0]
    vec = pl.BlockSpec((1, HEAD_DIM), lambda h, i: (0, 0))
    return pl.pallas_call(
        functools.partial(_attn_kernel, t=t),
        grid=(N_HEADS, s // t),
        in_specs=[vec, vec, vec, vec,
                  pl.BlockSpec((1, V_DIM), lambda h, i: (0, 0)),
                  pl.BlockSpec((t, V_DIM), lambda h, i: (i, h)),
                  pl.BlockSpec((s, V_DIM), lambda h, i: (0, N_HEADS + h)),
                  pl.BlockSpec((s, V_DIM), lambda h, i: (0, 2 * N_HEADS + h)),
                  pl.BlockSpec((t, V_DIM), lambda h, i: (i, h))],
        out_specs=pl.BlockSpec((t, V_DIM), lambda h, i: (i, h)),
        out_shape=jax.ShapeDtypeStruct((s, ATT_WIDTH), BF16),
        scratch_shapes=[pltpu.VMEM((t, V_DIM), F32), pltpu.VMEM((t, V_DIM), F32)],
        compiler_params=_params("parallel", "arbitrary"),
        name="diff_attention",
    )(lq1.reshape(1, -1), lk1.reshape(1, -1), lq2.reshape(1, -1), lk2.reshape(1, -1),
      g_subln.reshape(1, -1), qkv, qkv, qkv, rest)


def _conv_kernel(ga_ref, gb_ref, ga_halo_ref, gb_halo_ref, zc_ref, w_ref, b_ref, lng_ref, lnb_ref,
                 o_ref, ext_ref, y_ref, *, tr, lane_chunk, row_chunk):
    i = pl.program_id(0)
    halo = ga_halo_ref[...] * jax.nn.sigmoid(gb_halo_ref[...])
    ext_ref[0:CONV_HALO, :] = jnp.where(i > 0, halo, 0.0)
    ext_ref[CONV_HALO:CONV_HALO + tr, :] = ga_ref[...] * jax.nn.sigmoid(gb_ref[...])

    first_tap = CONV_HALO - (CONV_KERNEL - 1)

    def lane_body(c, _):
        c0 = pl.multiple_of(c * lane_chunk, lane_chunk)
        lanes = pl.ds(c0, lane_chunk)
        for r0 in range(0, tr, row_chunk):
            acc = jnp.broadcast_to(b_ref[:, lanes], (row_chunk, lane_chunk))
            for j in range(CONV_KERNEL):
                acc = acc + w_ref[j:j + 1, lanes] * ext_ref[pl.ds(first_tap + j + r0, row_chunk), lanes]
            y_ref[pl.ds(r0, row_chunk), lanes] = acc
        return 0

    lax.fori_loop(0, CONV_WIDTH // lane_chunk, lane_body, 0)

    y = y_ref[...]
    mu = jnp.mean(y, axis=-1, keepdims=True)
    yc = y - mu
    ln = yc * lax.rsqrt(jnp.mean(yc * yc, axis=-1, keepdims=True) + LN_EPS) * lng_ref[...] + lnb_ref[...]
    o_ref[...] = (_silu(ln) * _silu(zc_ref[...])).astype(o_ref.dtype)


def _conv_branch(rest, conv_w, conv_b, ln_g, ln_b, tr=256):
    s = rest.shape[0]
    c = CONV_WIDTH
    halo_blocks = tr // CONV_HALO
    row = lambda col: pl.BlockSpec((tr, c), lambda i: (i, col))
    halo = lambda col: pl.BlockSpec((CONV_HALO, c), lambda i: (jnp.maximum(i * halo_blocks - 1, 0), col))
    vec = pl.BlockSpec((1, c), lambda i: (0, 0))
    return pl.pallas_call(
        functools.partial(_conv_kernel, tr=tr, lane_chunk=128, row_chunk=128),
        grid=(s // tr,),
        in_specs=[row(1), row(2), halo(1), halo(2), row(3),
                  pl.BlockSpec((CONV_KERNEL, c), lambda i: (0, 0)), vec, vec, vec],
        out_specs=pl.BlockSpec((tr, c), lambda i: (i, 0)),
        out_shape=jax.ShapeDtypeStruct((s, c), BF16),
        scratch_shapes=[pltpu.VMEM((CONV_HALO + tr, c), F32), pltpu.VMEM((tr, c), F32)],
        compiler_params=_params("parallel"),
        name="conv_branch",
    )(rest, rest, rest, rest, rest, conv_w, conv_b.reshape(1, c), ln_g.reshape(1, c), ln_b.reshape(1, c))


def _merge_kernel(aa_ref, ac_ref, wa_ref, wc_ref, ga_ref, gc_ref, o_ref):
    ya = jnp.dot(aa_ref[...], wa_ref[...], preferred_element_type=F32)
    yc = jnp.dot(ac_ref[...], wc_ref[...], preferred_element_type=F32)
    m = jax.nn.sigmoid(ga_ref[...]) * ya + jax.nn.sigmoid(gc_ref[...]) * yc
    o_ref[...] = m.astype(o_ref.dtype)


def _merge(a_att, a_conv, w_att_out, w_conv_out, rest, tm=1024, tn=512):
    s, k = a_att.shape
    n = D_MODEL
    gate_a_block = (ATT_WIDTH + 3 * CONV_WIDTH) // tn
    gate_c_block = gate_a_block + D_MODEL // tn
    return pl.pallas_call(
        _merge_kernel,
        grid=(s // tm, n // tn),
        in_specs=[pl.BlockSpec((tm, k), lambda i, j: (i, 0)),
                  pl.BlockSpec((tm, k), lambda i, j: (i, 0)),
                  pl.BlockSpec((k, tn), lambda i, j: (0, j)),
                  pl.BlockSpec((k, tn), lambda i, j: (0, j)),
                  pl.BlockSpec((tm, tn), lambda i, j: (i, j + gate_a_block)),
                  pl.BlockSpec((tm, tn), lambda i, j: (i, j + gate_c_block))],
        out_specs=pl.BlockSpec((tm, tn), lambda i, j: (i, j)),
        out_shape=jax.ShapeDtypeStruct((s, n), BF16),
        compiler_params=_params("parallel", "arbitrary"),
        name="gated_merge",
    )(a_att, a_conv, w_att_out, w_conv_out, rest, rest)


def _outproj_kernel(m_ref, w_ref, x_ref, g_ref, x1_ref, hg_ref, *, tn):
    j = pl.program_id(1)
    cols = pl.ds(pl.multiple_of(j * tn, tn), tn)
    x1_ref[:, cols] = x_ref[...] + jnp.dot(m_ref[...], w_ref[...], preferred_element_type=F32)

    @pl.when(j == pl.num_programs(1) - 1)
    def _():
        x1 = x1_ref[...]
        hg_ref[...] = (x1 * _rms_scale(x1, RMS_EPS) * g_ref[...]).astype(hg_ref.dtype)


def _outproj(m, w_out, x, g_ple_gate, tm=512, tn=512):
    s, k = m.shape
    n = D_MODEL
    return pl.pallas_call(
        functools.partial(_outproj_kernel, tn=tn),
        grid=(s // tm, n // tn),
        in_specs=[pl.BlockSpec((tm, k), lambda i, j: (i, 0)),
                  pl.BlockSpec((k, tn), lambda i, j: (0, j)),
                  pl.BlockSpec((tm, tn), lambda i, j: (i, j)),
                  pl.BlockSpec((1, n), lambda i, j: (0, 0))],
        out_specs=[pl.BlockSpec((tm, n), lambda i, j: (i, 0)),
                   pl.BlockSpec((tm, n), lambda i, j: (i, 0))],
        out_shape=[jax.ShapeDtypeStruct((s, n), F32), jax.ShapeDtypeStruct((s, n), BF16)],
        compiler_params=_params("parallel", "arbitrary"),
        name="out_proj",
    )(m, w_out, x, g_ple_gate.reshape(1, n))


def _ple_kernel(hg_ref, wg_ref, x1_ref, p_ref, wp_ref, gpost_ref, gfin_ref, o_ref, *, tn):
    j = pl.program_id(1)

    @pl.when(j == 0)
    def _():
        e = jnp.dot(p_ref[...].astype(BF16), wp_ref[...], preferred_element_type=F32)
        o_ref[...] = e * _rms_scale(e, RMS_EPS) * gpost_ref[...]

    cols = pl.ds(pl.multiple_of(j * tn, tn), tn)
    g = jax.nn.sigmoid(jnp.dot(hg_ref[...], wg_ref[...], preferred_element_type=F32))
    o_ref[:, cols] = x1_ref[...] + g * o_ref[:, cols]

    @pl.when(j == pl.num_programs(1) - 1)
    def _():
        x2 = o_ref[...]
        o_ref[...] = x2 * _rms_scale(x2, RMS_EPS) * gfin_ref[...]


def _ple(hg, w_ple_gate, x1, p, w_ple, g_ple_post, g_final, tm=512, tn=512):
    s, k = hg.shape
    n = D_MODEL
    vec = pl.BlockSpec((1, n), lambda i, j: (0, 0))
    return pl.pallas_call(
        functools.partial(_ple_kernel, tn=tn),
        grid=(s // tm, n // tn),
        in_specs=[pl.BlockSpec((tm, k), lambda i, j: (i, 0)),
                  pl.BlockSpec((k, tn), lambda i, j: (0, j)),
                  pl.BlockSpec((tm, tn), lambda i, j: (i, j)),
                  pl.BlockSpec((tm, PLE_DIM), lambda i, j: (i, 0)),
                  pl.BlockSpec((PLE_DIM, n), lambda i, j: (0, 0)),
                  vec, vec],
        out_specs=pl.BlockSpec((tm, n), lambda i, j: (i, 0)),
        out_shape=jax.ShapeDtypeStruct((s, n), F32),
        compiler_params=_params("parallel", "arbitrary"),
        name="ple_final",
    )(hg, w_ple_gate, x1, p, w_ple, g_ple_post.reshape(1, n), g_final.reshape(1, n))


def kernel(x, p, g_mix, w_in, lambda_q1, lambda_k1, lambda_q2, lambda_k2, g_subln, w_att_out, conv_w,
           conv_b, ln_g, ln_b, w_conv_out, w_out, g_ple_gate, w_ple_gate, w_ple, g_ple_post, g_final):
    b, s, d = x.shape
    assert (b, s, d) == (1, SEQ, D_MODEL) and p.shape[0] == 1 and w_in.shape[0] == 1
    x2d = x.reshape(s, d)

    h = _rmsnorm(x2d, g_mix[0])
    w_in_b = w_in[0].astype(BF16)
    qkv = _matmul(h, w_in_b, 0, QKV_COLS, BF16, tm=1024, tn=1024, name="proj_qkv")
    rest = _matmul(h, w_in_b, QKV_COLS // 1024, REST_COLS, F32, tm=1024, tn=1024, name="proj_rest")

    a_att = _attention(qkv, rest, lambda_q1[0], lambda_k1[0], lambda_q2[0], lambda_k2[0], g_subln[0])
    a_conv = _conv_branch(rest, conv_w[0], conv_b[0], ln_g[0], ln_b[0])
    m = _merge(a_att, a_conv, w_att_out[0].astype(BF16), w_conv_out[0].astype(BF16), rest)
    x1, hg = _outproj(m, w_out[0].astype(BF16), x2d, g_ple_gate[0])
    out = _ple(hg, w_ple_gate[0].astype(BF16), x1, p[0, 0], w_ple[0].astype(BF16), g_ple_post[0], g_final)
    return out.reshape(b, s, d)
```

```python
import functools
import math

import jax
import jax.numpy as jnp
from jax import lax
from jax.experimental import pallas as pl
from jax.experimental.pallas import tpu as pltpu

D_MODEL = 4096
SEQ = 8192
PLE_DIM = 256
N_HEADS = 8
HEAD_DIM = 128
V_DIM = 2 * HEAD_DIM
ATT_WIDTH = N_HEADS * V_DIM
CONV_WIDTH = D_MODEL // 2
CONV_KERNEL = 31
QKV_COLS = 3 * ATT_WIDTH
REST_COLS = ATT_WIDTH + 3 * CONV_WIDTH + 2 * D_MODEL

RMS_EPS = 1e-6
SUBLN_EPS = 1e-5
LN_EPS = 1e-5
LAMBDA_INIT = 0.8 - 0.6 * math.exp(-0.3 * 0)
QK_LOG2_SCALE = HEAD_DIM ** -0.5 * math.log2(math.e)

VMEM_LIMIT_BYTES = 56 * 1024 * 1024
CONV_HALO = 32

BF16 = jnp.bfloat16
F32 = jnp.float32


def _params(*semantics):
    return pltpu.CompilerParams(dimension_semantics=semantics, vmem_limit_bytes=VMEM_LIMIT_BYTES)


def _rms_scale(v, eps):
    return lax.rsqrt(jnp.mean(v * v, axis=-1, keepdims=True) + eps)


def _silu(v):
    return v * jax.nn.sigmoid(v)


def _rmsnorm_kernel(x_ref, g_ref, o_ref):
    x = x_ref[...]
    o_ref[...] = (x * _rms_scale(x, RMS_EPS) * g_ref[...]).astype(o_ref.dtype)


def _rmsnorm(x, g, tm=512):
    s, d = x.shape
    return pl.pallas_call(
        _rmsnorm_kernel,
        grid=(s // tm,),
        in_specs=[pl.BlockSpec((tm, d), lambda i: (i, 0)),
                  pl.BlockSpec((1, d), lambda i: (0, 0))],
        out_specs=pl.BlockSpec((tm, d), lambda i: (i, 0)),
        out_shape=jax.ShapeDtypeStruct((s, d), BF16),
        compiler_params=_params("parallel"),
        name="rmsnorm_in",
    )(x, g.reshape(1, d))


def _matmul_kernel(a_ref, w_ref, o_ref, wb_ref, *, n_scaled_blocks, scale):
    @pl.when(pl.program_id(1) == 0)
    def _():
        wb_ref[...] = w_ref[...].astype(BF16)

    acc = jnp.dot(a_ref[...], wb_ref[...], preferred_element_type=F32)
    if n_scaled_blocks:
        acc = acc * jnp.where(pl.program_id(0) < n_scaled_blocks, scale, 1.0)
    o_ref[...] = acc.astype(o_ref.dtype)


def _matmul(a, w, col_block_offset, n_cols, out_dtype, tm, tn, name, n_scaled_blocks=0, scale=1.0):
    s, k = a.shape
    return pl.pallas_call(
        functools.partial(_matmul_kernel, n_scaled_blocks=n_scaled_blocks, scale=scale),
        grid=(n_cols // tn, s // tm),
        in_specs=[pl.BlockSpec((tm, k), lambda j, i: (i, 0)),
                  pl.BlockSpec((k, tn), lambda j, i: (0, j + col_block_offset))],
        out_specs=pl.BlockSpec((tm, tn), lambda j, i: (i, j)),
        out_shape=jax.ShapeDtypeStruct((s, n_cols), out_dtype),
        scratch_shapes=[pltpu.VMEM((k, tn), BF16)],
        compiler_params=_params("parallel", "arbitrary"),
        name=name,
    )(a, w)


def _attn_kernel(lq1_ref, lk1_ref, lq2_ref, lk2_ref, gsub_ref, q_ref, k_ref, vt_ref, za_ref,
                 o_ref, acc1_ref, acc2_ref, sa_ref, sb_ref, *, tq):
    tk = tq // 2
    qi = pl.program_id(1)
    lam = (jnp.exp(jnp.sum(lq1_ref[...] * lk1_ref[...], axis=-1, keepdims=True))
           - jnp.exp(jnp.sum(lq2_ref[...] * lk2_ref[...], axis=-1, keepdims=True))
           + LAMBDA_INIT)
    acc_refs = (acc1_ref, acc2_ref)
    acc1_ref[...] = jnp.zeros_like(acc1_ref)
    acc2_ref[...] = jnp.zeros_like(acc2_ref)

    def qk(c, s_ref):
        kc = k_ref[pl.ds(pl.multiple_of(c * tk, tk), tk), :]
        for comp in range(2):
            cols = slice(comp * HEAD_DIM, (comp + 1) * HEAD_DIM)
            s_ref[comp] = lax.dot_general(kc[:, cols], q_ref[:, cols], (((1,), (1,)), ((), ())),
                                          preferred_element_type=F32)

    def softmax_pv(c, s_ref, carry, masked):
        vtc = vt_ref[:, pl.ds(pl.multiple_of(c * tk, tk), tk)]
        new = []
        for comp in range(2):
            m, l = carry[comp]
            s = s_ref[comp]
            if masked:
                key = c * tk + lax.broadcasted_iota(jnp.int32, (tk, tq), 0)
                query = qi * tq + lax.broadcasted_iota(jnp.int32, (tk, tq), 1)
                s = jnp.where(key <= query, s, -jnp.inf)
            m_new = jnp.maximum(m, jnp.max(s, axis=0, keepdims=True))
            alpha = jnp.exp2(m - m_new)
            p = jnp.exp2(s - m_new)
            l_new = alpha * l + jnp.sum(p, axis=0, keepdims=True)
            acc_ref = acc_refs[comp]
            acc_ref[...] = alpha * acc_ref[...] + jnp.dot(vtc, p.astype(BF16), preferred_element_type=F32)
            new.append((m_new, l_new))
        return tuple(new)

    def pair(j, carry):
        c0 = 2 * j
        qk(c0 + 1, sb_ref)
        carry = softmax_pv(c0, sa_ref, carry, masked=False)
        qk(c0 + 2, sa_ref)
        return softmax_pv(c0 + 1, sb_ref, carry, masked=False)

    neg = jnp.full((1, tq), -jnp.inf, F32)
    zero = jnp.zeros((1, tq), F32)
    qk(0, sa_ref)
    carry = lax.fori_loop(0, qi, pair, ((neg, zero), (neg, zero)))
    qk(2 * qi + 1, sb_ref)
    carry = softmax_pv(2 * qi, sa_ref, carry, masked=True)
    (_, l1), (_, l2) = softmax_pv(2 * qi + 1, sb_ref, carry, masked=True)

    ot = acc1_ref[...] / l1 - lam * (acc2_ref[...] / l2)
    ot = ot * lax.rsqrt(jnp.mean(ot * ot, axis=0, keepdims=True) + SUBLN_EPS)
    o = ot.T * gsub_ref[...] * (1.0 - LAMBDA_INIT)
    o_ref[...] = (o * _silu(za_ref[...])).astype(o_ref.dtype)


def _attention(qkv, vt, rest, lq1, lk1, lq2, lk2, g_subln, tq=512):
    s = qkv.shape[0]
    vec = pl.BlockSpec((1, HEAD_DIM), lambda h, i: (0, 0))
    score_buf = pltpu.VMEM((2, tq // 2, tq), F32)
    return pl.pallas_call(
        functools.partial(_attn_kernel, tq=tq),
        grid=(N_HEADS, s // tq),
        in_specs=[vec, vec, vec, vec,
                  pl.BlockSpec((1, V_DIM), lambda h, i: (0, 0)),
                  pl.BlockSpec((tq, V_DIM), lambda h, i: (i, h)),
                  pl.BlockSpec((s, V_DIM), lambda h, i: (0, N_HEADS + h)),
                  pl.BlockSpec((V_DIM, s), lambda h, i: (h, 0)),
                  pl.BlockSpec((tq, V_DIM), lambda h, i: (i, h))],
        out_specs=pl.BlockSpec((tq, V_DIM), lambda h, i: (i, h)),
        out_shape=jax.ShapeDtypeStruct((s, ATT_WIDTH), BF16),
        scratch_shapes=[pltpu.VMEM((V_DIM, tq), F32), pltpu.VMEM((V_DIM, tq), F32), score_buf, score_buf],
        compiler_params=_params("parallel", "arbitrary"),
        name="diff_attention",
    )(lq1.reshape(1, -1), lk1.reshape(1, -1), lq2.reshape(1, -1), lk2.reshape(1, -1),
      g_subln.reshape(1, -1), qkv, qkv, vt, rest)


def _conv_kernel(ga_ref, gb_ref, ga_halo_ref, gb_halo_ref, zc_ref, w_ref, b_ref, lng_ref, lnb_ref,
                 o_ref, ext_ref, y_ref, *, tr, lane_chunk, row_chunk):
    i = pl.program_id(0)
    halo = ga_halo_ref[...] * jax.nn.sigmoid(gb_halo_ref[...])
    ext_ref[0:CONV_HALO, :] = jnp.where(i > 0, halo, 0.0)
    ext_ref[CONV_HALO:CONV_HALO + tr, :] = ga_ref[...] * jax.nn.sigmoid(gb_ref[...])

    first_tap = CONV_HALO - (CONV_KERNEL - 1)

    def lane_body(c, _):
        c0 = pl.multiple_of(c * lane_chunk, lane_chunk)
        lanes = pl.ds(c0, lane_chunk)
        for r0 in range(0, tr, row_chunk):
            acc = jnp.broadcast_to(b_ref[:, lanes], (row_chunk, lane_chunk))
            for j in range(CONV_KERNEL):
                acc = acc + w_ref[j:j + 1, lanes] * ext_ref[pl.ds(first_tap + j + r0, row_chunk), lanes]
            y_ref[pl.ds(r0, row_chunk), lanes] = acc
        return 0

    lax.fori_loop(0, CONV_WIDTH // lane_chunk, lane_body, 0)

    y = y_ref[...]
    mu = jnp.mean(y, axis=-1, keepdims=True)
    yc = y - mu
    ln = yc * lax.rsqrt(jnp.mean(yc * yc, axis=-1, keepdims=True) + LN_EPS) * lng_ref[...] + lnb_ref[...]
    o_ref[...] = (_silu(ln) * _silu(zc_ref[...])).astype(o_ref.dtype)


def _conv_branch(rest, conv_w, conv_b, ln_g, ln_b, tr=256):
    s = rest.shape[0]
    c = CONV_WIDTH
    halo_blocks = tr // CONV_HALO
    row = lambda col: pl.BlockSpec((tr, c), lambda i: (i, col))
    halo = lambda col: pl.BlockSpec((CONV_HALO, c), lambda i: (jnp.maximum(i * halo_blocks - 1, 0), col))
    vec = pl.BlockSpec((1, c), lambda i: (0, 0))
    return pl.pallas_call(
        functools.partial(_conv_kernel, tr=tr, lane_chunk=128, row_chunk=128),
        grid=(s // tr,),
        in_specs=[row(1), row(2), halo(1), halo(2), row(3),
                  pl.BlockSpec((CONV_KERNEL, c), lambda i: (0, 0)), vec, vec, vec],
        out_specs=pl.BlockSpec((tr, c), lambda i: (i, 0)),
        out_shape=jax.ShapeDtypeStruct((s, c), BF16),
        scratch_shapes=[pltpu.VMEM((CONV_HALO + tr, c), F32), pltpu.VMEM((tr, c), F32)],
        compiler_params=_params("parallel"),
        name="conv_branch",
    )(rest, rest, rest, rest, rest, conv_w, conv_b.reshape(1, c), ln_g.reshape(1, c), ln_b.reshape(1, c))


def _merge_kernel(aa_ref, ac_ref, wa_ref, wc_ref, ga_ref, gc_ref, o_ref, wab_ref, wcb_ref):
    @pl.when(pl.program_id(1) == 0)
    def _():
        wab_ref[...] = wa_ref[...].astype(BF16)
        wcb_ref[...] = wc_ref[...].astype(BF16)

    ya = jnp.dot(aa_ref[...], wab_ref[...], preferred_element_type=F32)
    yc = jnp.dot(ac_ref[...], wcb_ref[...], preferred_element_type=F32)
    m = jax.nn.sigmoid(ga_ref[...]) * ya + jax.nn.sigmoid(gc_ref[...]) * yc
    o_ref[...] = m.astype(o_ref.dtype)


def _merge(a_att, a_conv, w_att_out, w_conv_out, rest, tm=1024, tn=512):
    s, k = a_att.shape
    n = D_MODEL
    gate_a_block = (ATT_WIDTH + 3 * CONV_WIDTH) // tn
    gate_c_block = gate_a_block + D_MODEL // tn
    return pl.pallas_call(
        _merge_kernel,
        grid=(n // tn, s // tm),
        in_specs=[pl.BlockSpec((tm, k), lambda j, i: (i, 0)),
                  pl.BlockSpec((tm, k), lambda j, i: (i, 0)),
                  pl.BlockSpec((k, tn), lambda j, i: (0, j)),
                  pl.BlockSpec((k, tn), lambda j, i: (0, j)),
                  pl.BlockSpec((tm, tn), lambda j, i: (i, j + gate_a_block)),
                  pl.BlockSpec((tm, tn), lambda j, i: (i, j + gate_c_block))],
        out_specs=pl.BlockSpec((tm, tn), lambda j, i: (i, j)),
        out_shape=jax.ShapeDtypeStruct((s, n), BF16),
        scratch_shapes=[pltpu.VMEM((k, tn), BF16), pltpu.VMEM((k, tn), BF16)],
        compiler_params=_params("parallel", "arbitrary"),
        name="gated_merge",
    )(a_att, a_conv, w_att_out, w_conv_out, rest, rest)


def _outproj_kernel(m_ref, w_ref, x_ref, g_ref, x1_ref, hg_ref, *, tn):
    j = pl.program_id(1)
    cols = pl.ds(pl.multiple_of(j * tn, tn), tn)
    x1_ref[:, cols] = x_ref[...] + jnp.dot(m_ref[...], w_ref[...], preferred_element_type=F32)

    @pl.when(j == pl.num_programs(1) - 1)
    def _():
        x1 = x1_ref[...]
        hg_ref[...] = (x1 * _rms_scale(x1, RMS_EPS) * g_ref[...]).astype(hg_ref.dtype)


def _outproj(m, w_out, x, g_ple_gate, tm=512, tn=512):
    s, k = m.shape
    n = D_MODEL
    return pl.pallas_call(
        functools.partial(_outproj_kernel, tn=tn),
        grid=(s // tm, n // tn),
        in_specs=[pl.BlockSpec((tm, k), lambda i, j: (i, 0)),
                  pl.BlockSpec((k, tn), lambda i, j: (0, j)),
                  pl.BlockSpec((tm, tn), lambda i, j: (i, j)),
                  pl.BlockSpec((1, n), lambda i, j: (0, 0))],
        out_specs=[pl.BlockSpec((tm, n), lambda i, j: (i, 0)),
                   pl.BlockSpec((tm, n), lambda i, j: (i, 0))],
        out_shape=[jax.ShapeDtypeStruct((s, n), F32), jax.ShapeDtypeStruct((s, n), BF16)],
        compiler_params=_params("parallel", "arbitrary"),
        name="out_proj",
    )(m, w_out, x, g_ple_gate.reshape(1, n))


def _ple_kernel(hg_ref, wg_ref, x1_ref, p_ref, wp_ref, gpost_ref, gfin_ref, o_ref, *, tn):
    j = pl.program_id(1)

    @pl.when(j == 0)
    def _():
        e = jnp.dot(p_ref[...].astype(BF16), wp_ref[...], preferred_element_type=F32)
        o_ref[...] = e * _rms_scale(e, RMS_EPS) * gpost_ref[...]

    cols = pl.ds(pl.multiple_of(j * tn, tn), tn)
    g = jax.nn.sigmoid(jnp.dot(hg_ref[...], wg_ref[...], preferred_element_type=F32))
    o_ref[:, cols] = x1_ref[...] + g * o_ref[:, cols]

    @pl.when(j == pl.num_programs(1) - 1)
    def _():
        x2 = o_ref[...]
        o_ref[...] = x2 * _rms_scale(x2, RMS_EPS) * gfin_ref[...]


def _ple(hg, w_ple_gate, x1, p, w_ple, g_ple_post, g_final, tm=512, tn=512):
    s, k = hg.shape
    n = D_MODEL
    vec = pl.BlockSpec((1, n), lambda i, j: (0, 0))
    return pl.pallas_call(
        functools.partial(_ple_kernel, tn=tn),
        grid=(s // tm, n // tn),
        in_specs=[pl.BlockSpec((tm, k), lambda i, j: (i, 0)),
                  pl.BlockSpec((k, tn), lambda i, j: (0, j)),
                  pl.BlockSpec((tm, tn), lambda i, j: (i, j)),
                  pl.BlockSpec((tm, PLE_DIM), lambda i, j: (i, 0)),
                  pl.BlockSpec((PLE_DIM, n), lambda i, j: (0, 0)),
                  vec, vec],
        out_specs=pl.BlockSpec((tm, n), lambda i, j: (i, 0)),
        out_shape=jax.ShapeDtypeStruct((s, n), F32),
        compiler_params=_params("parallel", "arbitrary"),
        name="ple_final",
    )(hg, w_ple_gate, x1, p, w_ple, g_ple_post.reshape(1, n), g_final.reshape(1, n))


def kernel(x, p, g_mix, w_in, lambda_q1, lambda_k1, lambda_q2, lambda_k2, g_subln, w_att_out, conv_w,
           conv_b, ln_g, ln_b, w_conv_out, w_out, g_ple_gate, w_ple_gate, w_ple, g_ple_post, g_final):
    b, s, d = x.shape
    assert (b, s, d) == (1, SEQ, D_MODEL) and p.shape[0] == 1 and w_in.shape[0] == 1
    x2d = x.reshape(s, d)

    h = _rmsnorm(x2d, g_mix[0])
    tn = 1024
    qkv = _matmul(h, w_in[0], 0, QKV_COLS, BF16, tm=512, tn=tn, name="proj_qkv",
                  n_scaled_blocks=ATT_WIDTH // tn, scale=QK_LOG2_SCALE)
    rest = _matmul(h, w_in[0], QKV_COLS // tn, REST_COLS, F32, tm=512, tn=tn, name="proj_rest")

    vt = qkv[:, 2 * ATT_WIDTH:].T
    a_att = _attention(qkv, vt, rest, lambda_q1[0], lambda_k1[0], lambda_q2[0], lambda_k2[0], g_subln[0])
    a_conv = _conv_branch(rest, conv_w[0], conv_b[0], ln_g[0], ln_b[0])
    m = _merge(a_att, a_conv, w_att_out[0], w_conv_out[0], rest)
    x1, hg = _outproj(m, w_out[0].astype(BF16), x2d, g_ple_gate[0])
    out = _ple(hg, w_ple_gate[0].astype(BF16), x1, p[0, 0], w_ple[0].astype(BF16), g_ple_post[0], g_final)
    return out.reshape(b, s, d)
```

```python
import functools
import math

import jax
import jax.numpy as jnp
from jax import lax
from jax.experimental import pallas as pl
from jax.experimental.pallas import tpu as pltpu

D_MODEL = 4096
SEQ = 8192
PLE_DIM = 256
N_HEADS = 8
HEAD_DIM = 128
V_DIM = 2 * HEAD_DIM
ATT_WIDTH = N_HEADS * V_DIM
CONV_WIDTH = D_MODEL // 2
CONV_KERNEL = 31
QKV_COLS = 3 * ATT_WIDTH
REST_COLS = ATT_WIDTH + 3 * CONV_WIDTH + 2 * D_MODEL

RMS_EPS = 1e-6
SUBLN_EPS = 1e-5
LN_EPS = 1e-5
LAMBDA_INIT = 0.8 - 0.6 * math.exp(-0.3 * 0)
QK_LOG2_SCALE = HEAD_DIM ** -0.5 * math.log2(math.e)

VMEM_LIMIT_BYTES = 56 * 1024 * 1024
CONV_HALO = 32

BF16 = jnp.bfloat16
F32 = jnp.float32


def _params(*semantics):
    return pltpu.CompilerParams(dimension_semantics=semantics, vmem_limit_bytes=VMEM_LIMIT_BYTES)


def _rms_scale(v, eps):
    return lax.rsqrt(jnp.mean(v * v, axis=-1, keepdims=True) + eps)


def _silu(v):
    return v * jax.nn.sigmoid(v)


def _for_row_chunks(n_rows, chunk, body):
    for r in range(0, n_rows, chunk):
        body(slice(r, r + chunk))


def _rmsnorm_kernel(x_ref, g_ref, o_ref):
    x = x_ref[...]
    o_ref[...] = (x * _rms_scale(x, RMS_EPS) * g_ref[...]).astype(o_ref.dtype)


def _rmsnorm(x, g, tm=512):
    s, d = x.shape
    return pl.pallas_call(
        _rmsnorm_kernel,
        grid=(s // tm,),
        in_specs=[pl.BlockSpec((tm, d), lambda i: (i, 0)),
                  pl.BlockSpec((1, d), lambda i: (0, 0))],
        out_specs=pl.BlockSpec((tm, d), lambda i: (i, 0)),
        out_shape=jax.ShapeDtypeStruct((s, d), BF16),
        compiler_params=_params("parallel"),
        name="rmsnorm_in",
    )(x, g.reshape(1, d))


def _matmul_kernel(a_ref, w_hbm_ref, o_ref, wf_ref, wb_ref, sem, *, col_block_offset, tn, n_scaled_blocks, scale):
    j = pl.program_id(0)

    def weight_copy(block):
        return pltpu.make_async_copy(
            w_hbm_ref.at[:, pl.ds(pl.multiple_of((block + col_block_offset) * tn, tn), tn)], wf_ref, sem)

    @pl.when(pl.program_id(1) == 0)
    def _():
        @pl.when(j == 0)
        def _():
            weight_copy(0).start()

        weight_copy(j).wait()
        wb_ref[...] = wf_ref[...].astype(BF16)

        @pl.when(j + 1 < pl.num_programs(0))
        def _():
            weight_copy(j + 1).start()

    acc = jnp.dot(a_ref[...], wb_ref[...], preferred_element_type=F32)
    if n_scaled_blocks:
        acc = acc * jnp.where(j < n_scaled_blocks, scale, 1.0)
    o_ref[...] = acc.astype(o_ref.dtype)


def _matmul(a, w, col_block_offset, n_cols, out_dtype, tm, tn, name, n_scaled_blocks=0, scale=1.0):
    s, k = a.shape
    return pl.pallas_call(
        functools.partial(_matmul_kernel, col_block_offset=col_block_offset, tn=tn,
                          n_scaled_blocks=n_scaled_blocks, scale=scale),
        grid=(n_cols // tn, s // tm),
        in_specs=[pl.BlockSpec((tm, k), lambda j, i: (i, 0)),
                  pl.BlockSpec(memory_space=pl.ANY)],
        out_specs=pl.BlockSpec((tm, tn), lambda j, i: (i, j)),
        out_shape=jax.ShapeDtypeStruct((s, n_cols), out_dtype),
        scratch_shapes=[pltpu.VMEM((k, tn), F32), pltpu.VMEM((k, tn), BF16), pltpu.SemaphoreType.DMA(())],
        compiler_params=_params("arbitrary", "arbitrary"),
        name=name,
    )(a, w)


def _attn_kernel(lq1_ref, lk1_ref, lq2_ref, lk2_ref, gsub_ref, qt_ref, k_ref, vt_ref, za_ref,
                 o_ref, acc1_ref, acc2_ref, sa_ref, sb_ref, *, tq):
    tk = tq // 2
    qi = pl.program_id(1)
    lam = (jnp.exp(jnp.sum(lq1_ref[...] * lk1_ref[...], axis=-1, keepdims=True))
           - jnp.exp(jnp.sum(lq2_ref[...] * lk2_ref[...], axis=-1, keepdims=True))
           + LAMBDA_INIT)
    acc_refs = (acc1_ref, acc2_ref)
    acc1_ref[...] = jnp.zeros_like(acc1_ref)
    acc2_ref[...] = jnp.zeros_like(acc2_ref)

    def qk(c, s_ref, q0=0):
        kc = k_ref[pl.ds(pl.multiple_of(c * tk, tk), tk), :]
        for comp in range(2):
            dims = slice(comp * HEAD_DIM, (comp + 1) * HEAD_DIM)
            s_ref[comp, :, q0:] = jnp.dot(kc[:, dims], qt_ref[dims, q0:], preferred_element_type=F32)

    def softmax_pv(c, s_ref, carry, masked, q0=0):
        vtc = vt_ref[:, pl.ds(pl.multiple_of(c * tk, tk), tk)]
        new = []
        for comp in range(2):
            m, l = carry[comp]
            s = s_ref[comp, :, q0:]
            if masked:
                key = c * tk + lax.broadcasted_iota(jnp.int32, s.shape, 0)
                query = qi * tq + q0 + lax.broadcasted_iota(jnp.int32, s.shape, 1)
                s = jnp.where(key <= query, s, -jnp.inf)
            m_new = jnp.maximum(m[:, q0:], jnp.max(s, axis=0, keepdims=True))
            alpha = jnp.exp2(m[:, q0:] - m_new)
            p = jnp.exp2(s - m_new)
            l_new = alpha * l[:, q0:] + jnp.sum(p, axis=0, keepdims=True)
            acc_ref = acc_refs[comp]
            acc_ref[:, q0:] = alpha * acc_ref[:, q0:] + jnp.dot(vtc, p.astype(BF16), preferred_element_type=F32)
            if q0:
                m_new = jnp.concatenate([m[:, :q0], m_new], axis=1)
                l_new = jnp.concatenate([l[:, :q0], l_new], axis=1)
            new.append((m_new, l_new))
        return tuple(new)

    def pair(j, carry):
        c0 = 2 * j
        qk(c0 + 1, sb_ref)
        carry = softmax_pv(c0, sa_ref, carry, masked=False)
        qk(c0 + 2, sa_ref)
        return softmax_pv(c0 + 1, sb_ref, carry, masked=False)

    neg = jnp.full((1, tq), -jnp.inf, F32)
    zero = jnp.zeros((1, tq), F32)
    qk(0, sa_ref)
    carry = lax.fori_loop(0, qi, pair, ((neg, zero), (neg, zero)))
    qk(2 * qi + 1, sb_ref, q0=tk)
    carry = softmax_pv(2 * qi, sa_ref, carry, masked=True)
    (_, l1), (_, l2) = softmax_pv(2 * qi + 1, sb_ref, carry, masked=True, q0=tk)

    ot = acc1_ref[...] / l1 - lam * (acc2_ref[...] / l2)
    ot = ot * lax.rsqrt(jnp.mean(ot * ot, axis=0, keepdims=True) + SUBLN_EPS)
    o = ot.T * gsub_ref[...] * (1.0 - LAMBDA_INIT)
    o_ref[...] = (o * _silu(za_ref[...])).astype(o_ref.dtype)


def _attention(qkv, qt, vt, rest, lq1, lk1, lq2, lk2, g_subln, tq=1024):
    s = qkv.shape[0]
    vec = pl.BlockSpec((1, HEAD_DIM), lambda h, i: (0, 0))
    score_buf = pltpu.VMEM((2, tq // 2, tq), F32)
    return pl.pallas_call(
        functools.partial(_attn_kernel, tq=tq),
        grid=(N_HEADS, s // tq),
        in_specs=[vec, vec, vec, vec,
                  pl.BlockSpec((1, V_DIM), lambda h, i: (0, 0)),
                  pl.BlockSpec((V_DIM, tq), lambda h, i: (h, i)),
                  pl.BlockSpec((s, V_DIM), lambda h, i: (0, N_HEADS + h)),
                  pl.BlockSpec((V_DIM, s), lambda h, i: (h, 0)),
                  pl.BlockSpec((tq, V_DIM), lambda h, i: (i, h))],
        out_specs=pl.BlockSpec((tq, V_DIM), lambda h, i: (i, h)),
        out_shape=jax.ShapeDtypeStruct((s, ATT_WIDTH), BF16),
        scratch_shapes=[pltpu.VMEM((V_DIM, tq), F32), pltpu.VMEM((V_DIM, tq), F32), score_buf, score_buf],
        compiler_params=_params("parallel", "arbitrary"),
        name="diff_attention",
    )(lq1.reshape(1, -1), lk1.reshape(1, -1), lq2.reshape(1, -1), lk2.reshape(1, -1),
      g_subln.reshape(1, -1), qt, qkv, vt, rest)


def _conv_kernel(ga_ref, gb_ref, ga_halo_ref, gb_halo_ref, zc_ref, w_ref, b_ref, lng_ref, lnb_ref,
                 o_ref, ext_ref, y_ref, *, tr, lane_chunk, row_chunk):
    i = pl.program_id(0)
    halo = ga_halo_ref[...] * jax.nn.sigmoid(gb_halo_ref[...])
    ext_ref[0:CONV_HALO, :] = jnp.where(i > 0, halo, 0.0)

    def glu(rows):
        ext_ref[CONV_HALO + rows.start:CONV_HALO + rows.stop, :] = ga_ref[rows, :] * jax.nn.sigmoid(gb_ref[rows, :])
    _for_row_chunks(tr, 16, glu)

    first_tap = CONV_HALO - (CONV_KERNEL - 1)
    slab_rows = row_chunk + CONV_HALO

    def lane_body(c, _):
        lanes = pl.ds(pl.multiple_of(c * lane_chunk, lane_chunk), lane_chunk)
        for r0 in range(0, tr, row_chunk):
            slab = ext_ref[r0:r0 + slab_rows, lanes]
            acc = jnp.broadcast_to(b_ref[:, lanes], (row_chunk, lane_chunk))
            for phase in range(8):
                shifted = slab if phase == 0 else pltpu.roll(slab, slab_rows - phase, 0)
                for j in range(CONV_KERNEL):
                    if (first_tap + j) % 8 == phase:
                        off = first_tap + j - phase
                        acc = acc + w_ref[j:j + 1, lanes] * shifted[off:off + row_chunk]
            y_ref[r0:r0 + row_chunk, lanes] = acc
        return 0

    lax.fori_loop(0, CONV_WIDTH // lane_chunk, lane_body, 0)

    def norm_gate(rows):
        y = y_ref[rows, :]
        yc = y - jnp.mean(y, axis=-1, keepdims=True)
        ln = yc * lax.rsqrt(jnp.mean(yc * yc, axis=-1, keepdims=True) + LN_EPS) * lng_ref[...] + lnb_ref[...]
        o_ref[rows, :] = (_silu(ln) * _silu(zc_ref[rows, :])).astype(o_ref.dtype)
    _for_row_chunks(tr, 16, norm_gate)


def _conv_branch(rest, conv_w, conv_b, ln_g, ln_b, tr=256):
    s = rest.shape[0]
    c = CONV_WIDTH
    halo_blocks = tr // CONV_HALO
    row = lambda col: pl.BlockSpec((tr, c), lambda i: (i, col))
    halo = lambda col: pl.BlockSpec((CONV_HALO, c), lambda i: (jnp.maximum(i * halo_blocks - 1, 0), col))
    vec = pl.BlockSpec((1, c), lambda i: (0, 0))
    return pl.pallas_call(
        functools.partial(_conv_kernel, tr=tr, lane_chunk=128, row_chunk=128),
        grid=(s // tr,),
        in_specs=[row(1), row(2), halo(1), halo(2), row(3),
                  pl.BlockSpec((CONV_KERNEL, c), lambda i: (0, 0)), vec, vec, vec],
        out_specs=pl.BlockSpec((tr, c), lambda i: (i, 0)),
        out_shape=jax.ShapeDtypeStruct((s, c), BF16),
        scratch_shapes=[pltpu.VMEM((CONV_HALO + tr, c), F32), pltpu.VMEM((tr, c), F32)],
        compiler_params=_params("parallel"),
        name="conv_branch",
    )(rest, rest, rest, rest, rest, conv_w, conv_b.reshape(1, c), ln_g.reshape(1, c), ln_b.reshape(1, c))


def _merge_kernel(aa_ref, ac_ref, wa_ref, wc_ref, ga_ref, gc_ref, o_ref, wab_ref, wcb_ref):
    @pl.when(pl.program_id(1) == 0)
    def _():
        wab_ref[...] = wa_ref[...].astype(BF16)
        wcb_ref[...] = wc_ref[...].astype(BF16)

    ya = jnp.dot(aa_ref[...], wab_ref[...], preferred_element_type=F32)
    yc = jnp.dot(ac_ref[...], wcb_ref[...], preferred_element_type=F32)
    m = jax.nn.sigmoid(ga_ref[...]) * ya + jax.nn.sigmoid(gc_ref[...]) * yc
    o_ref[...] = m.astype(o_ref.dtype)


def _merge(a_att, a_conv, w_att_out, w_conv_out, rest, tm=1024, tn=512):
    s, k = a_att.shape
    n = D_MODEL
    gate_a_block = (ATT_WIDTH + 3 * CONV_WIDTH) // tn
    gate_c_block = gate_a_block + D_MODEL // tn
    return pl.pallas_call(
        _merge_kernel,
        grid=(n // tn, s // tm),
        in_specs=[pl.BlockSpec((tm, k), lambda j, i: (i, 0)),
                  pl.BlockSpec((tm, k), lambda j, i: (i, 0)),
                  pl.BlockSpec((k, tn), lambda j, i: (0, j)),
                  pl.BlockSpec((k, tn), lambda j, i: (0, j)),
                  pl.BlockSpec((tm, tn), lambda j, i: (i, j + gate_a_block)),
                  pl.BlockSpec((tm, tn), lambda j, i: (i, j + gate_c_block))],
        out_specs=pl.BlockSpec((tm, tn), lambda j, i: (i, j)),
        out_shape=jax.ShapeDtypeStruct((s, n), BF16),
        scratch_shapes=[pltpu.VMEM((k, tn), BF16), pltpu.VMEM((k, tn), BF16)],
        compiler_params=_params("parallel", "arbitrary"),
        name="gated_merge",
    )(a_att, a_conv, w_att_out, w_conv_out, rest, rest)


def _outproj_kernel(m_ref, w_ref, x_ref, o_ref, wb_ref):
    @pl.when(pl.program_id(1) == 0)
    def _():
        wb_ref[...] = w_ref[...].astype(BF16)

    o_ref[...] = x_ref[...] + jnp.dot(m_ref[...], wb_ref[...], preferred_element_type=F32)


def _outproj(m, w_out, x, tm=1024, tn=512):
    s, k = m.shape
    n = D_MODEL
    return pl.pallas_call(
        _outproj_kernel,
        grid=(n // tn, s // tm),
        in_specs=[pl.BlockSpec((tm, k), lambda j, i: (i, 0)),
                  pl.BlockSpec((k, tn), lambda j, i: (0, j)),
                  pl.BlockSpec((tm, tn), lambda j, i: (i, j))],
        out_specs=pl.BlockSpec((tm, tn), lambda j, i: (i, j)),
        out_shape=jax.ShapeDtypeStruct((s, n), F32),
        scratch_shapes=[pltpu.VMEM((k, tn), BF16)],
        compiler_params=_params("parallel", "arbitrary"),
        name="out_proj",
    )(m, w_out, x)


def _ple_kernel(x1_ref, wg_ref, p_ref, wp_ref, ggate_ref, gpost_ref, gfin_ref, o_ref, hg_ref, *, tm, tn):
    def gate_norm(rows):
        xr = x1_ref[rows, :]
        hg_ref[rows, :] = (xr * _rms_scale(xr, RMS_EPS) * ggate_ref[...]).astype(BF16)
    _for_row_chunks(tm, 16, gate_norm)

    o_ref[...] = jnp.dot(p_ref[...].astype(BF16), wp_ref[...], preferred_element_type=F32)

    def embed_norm(rows):
        er = o_ref[rows, :]
        o_ref[rows, :] = er * _rms_scale(er, RMS_EPS) * gpost_ref[...]
    _for_row_chunks(tm, 8, embed_norm)

    for c in range(D_MODEL // tn):
        cols = slice(c * tn, (c + 1) * tn)
        g = jax.nn.sigmoid(jnp.dot(hg_ref[...], wg_ref[:, cols], preferred_element_type=F32))
        o_ref[:, cols] = x1_ref[:, cols] + g * o_ref[:, cols]

    def final_norm(rows):
        xr = o_ref[rows, :]
        o_ref[rows, :] = xr * _rms_scale(xr, RMS_EPS) * gfin_ref[...]
    _for_row_chunks(tm, 8, final_norm)


def _ple(x1, w_ple_gate, p, w_ple, g_ple_gate, g_ple_post, g_final, tm=256, tn=512):
    s, n = x1.shape
    vec = pl.BlockSpec((1, n), lambda i: (0, 0))
    resident = pl.Buffered(1)
    return pl.pallas_call(
        functools.partial(_ple_kernel, tm=tm, tn=tn),
        grid=(s // tm,),
        in_specs=[pl.BlockSpec((tm, n), lambda i: (i, 0)),
                  pl.BlockSpec((n, n), lambda i: (0, 0), pipeline_mode=resident),
                  pl.BlockSpec((tm, PLE_DIM), lambda i: (i, 0)),
                  pl.BlockSpec((PLE_DIM, n), lambda i: (0, 0), pipeline_mode=resident),
                  vec, vec, vec],
        out_specs=pl.BlockSpec((tm, n), lambda i: (i, 0)),
        out_shape=jax.ShapeDtypeStruct((s, n), F32),
        scratch_shapes=[pltpu.VMEM((tm, n), BF16)],
        compiler_params=_params("parallel"),
        name="ple_final",
    )(x1, w_ple_gate, p, w_ple, g_ple_gate.reshape(1, n), g_ple_post.reshape(1, n), g_final.reshape(1, n))


def kernel(x, p, g_mix, w_in, lambda_q1, lambda_k1, lambda_q2, lambda_k2, g_subln, w_att_out, conv_w,
           conv_b, ln_g, ln_b, w_conv_out, w_out, g_ple_gate, w_ple_gate, w_ple, g_ple_post, g_final):
    b, s, d = x.shape
    assert (b, s, d) == (1, SEQ, D_MODEL) and p.shape[0] == 1 and w_in.shape[0] == 1
    x2d = x.reshape(s, d)

    h = _rmsnorm(x2d, g_mix[0])
    tn = 1024
    qkv = _matmul(h, w_in[0], 0, QKV_COLS, BF16, tm=1024, tn=tn, name="proj_qkv",
                  n_scaled_blocks=ATT_WIDTH // tn, scale=QK_LOG2_SCALE)
    rest = _matmul(h, w_in[0], QKV_COLS // tn, REST_COLS, F32, tm=1024, tn=tn, name="proj_rest")

    qt = qkv[:, :ATT_WIDTH].T
    vt = qkv[:, 2 * ATT_WIDTH:].T
    a_att = _attention(qkv, qt, vt, rest, lambda_q1[0], lambda_k1[0], lambda_q2[0], lambda_k2[0], g_subln[0])
    a_conv = _conv_branch(rest, conv_w[0], conv_b[0], ln_g[0], ln_b[0])
    m = _merge(a_att, a_conv, w_att_out[0], w_conv_out[0], rest)
    x1 = _outproj(m, w_out[0], x2d)
    out = _ple(x1, w_ple_gate[0].astype(BF16), p[0, 0], w_ple[0].astype(BF16),
               g_ple_gate[0], g_ple_post[0], g_final)
    return out.reshape(b, s, d)
```

```python
import functools
import math

import jax
import jax.numpy as jnp
from jax import lax
from jax.experimental import pallas as pl
from jax.experimental.pallas import tpu as pltpu

D_MODEL = 4096
SEQ = 8192
PLE_DIM = 256
N_HEADS = 8
HEAD_DIM = 128
V_DIM = 2 * HEAD_DIM
ATT_WIDTH = N_HEADS * V_DIM
CONV_WIDTH = D_MODEL // 2
CONV_KERNEL = 31

RMS_EPS = 1e-6
SUBLN_EPS = 1e-5
LN_EPS = 1e-5
LAMBDA_INIT = 0.8 - 0.6 * math.exp(-0.3 * 0)
QK_LOG2_SCALE = HEAD_DIM ** -0.5 * math.log2(math.e)

VMEM_LIMIT_BYTES = 56 * 1024 * 1024
CONV_HALO = 32

BF16 = jnp.bfloat16
F32 = jnp.float32


def _params(*semantics):
    return pltpu.CompilerParams(dimension_semantics=semantics, vmem_limit_bytes=VMEM_LIMIT_BYTES)


def _rms_scale(v, eps):
    return lax.rsqrt(jnp.mean(v * v, axis=-1, keepdims=True) + eps)


def _silu(v):
    return v * jax.nn.sigmoid(v)


def _for_row_chunks(n_rows, chunk, body):
    for r in range(0, n_rows, chunk):
        body(slice(r, r + chunk))


def _rmsnorm_kernel(x_ref, g_ref, o_ref):
    x = x_ref[...]
    o_ref[...] = (x * _rms_scale(x, RMS_EPS) * g_ref[...]).astype(o_ref.dtype)


def _rmsnorm(x, g, tm=512):
    s, d = x.shape
    return pl.pallas_call(
        _rmsnorm_kernel,
        grid=(s // tm,),
        in_specs=[pl.BlockSpec((tm, d), lambda i: (i, 0)),
                  pl.BlockSpec((1, d), lambda i: (0, 0))],
        out_specs=pl.BlockSpec((tm, d), lambda i: (i, 0)),
        out_shape=jax.ShapeDtypeStruct((s, d), BF16),
        compiler_params=_params("parallel"),
        name="rmsnorm_in",
    )(x, g.reshape(1, d))


def _weight_col_block(j, col_blocks):
    block = j + col_blocks[0]
    for pos in range(1, len(col_blocks)):
        gap = col_blocks[pos] - col_blocks[pos - 1] - 1
        if gap:
            block = block + jnp.where(j >= pos, gap, 0)
    return block


def _stage_weight_block(w_hbm_ref, wf_ref, wb_ref, sem, col_blocks, tn):
    j = pl.program_id(0)

    def weight_copy(step):
        col = pl.multiple_of(_weight_col_block(step, col_blocks) * tn, tn)
        return pltpu.make_async_copy(w_hbm_ref.at[:, pl.ds(col, tn)], wf_ref, sem)

    @pl.when(pl.program_id(1) == 0)
    def _():
        @pl.when(j == 0)
        def _():
            weight_copy(0).start()

        weight_copy(j).wait()
        wb_ref[...] = wf_ref[...].astype(BF16)

        @pl.when(j + 1 < pl.num_programs(0))
        def _():
            weight_copy(j + 1).start()


def _weight_stage_scratch(k, tn):
    return [pltpu.VMEM((k, tn), F32), pltpu.VMEM((k, tn), BF16), pltpu.SemaphoreType.DMA(())]


def _matmul_kernel(*refs, col_blocks, tn, n_scaled_blocks, scale, has_residual, transpose_out):
    if has_residual:
        a_ref, w_hbm_ref, x_ref, o_ref, wf_ref, wb_ref, sem = refs
    else:
        a_ref, w_hbm_ref, o_ref, wf_ref, wb_ref, sem = refs
    _stage_weight_block(w_hbm_ref, wf_ref, wb_ref, sem, col_blocks, tn)
    acc = jnp.dot(a_ref[...], wb_ref[...], preferred_element_type=F32)
    if n_scaled_blocks:
        acc = acc * jnp.where(pl.program_id(0) < n_scaled_blocks, scale, 1.0)
    if has_residual:
        acc = x_ref[...] + acc
    if transpose_out:
        acc = acc.T
    o_ref[...] = acc.astype(o_ref.dtype)


def _matmul(a, w, col_blocks, out_dtype, tm, tn, name, n_scaled_blocks=0, scale=1.0, residual=None,
            transpose_out=False):
    s, k = a.shape
    n_cols = len(col_blocks) * tn
    tile = pl.BlockSpec((tm, tn), lambda j, i: (i, j))
    in_specs = [pl.BlockSpec((tm, k), lambda j, i: (i, 0)), pl.BlockSpec(memory_space=pl.ANY)]
    operands = [a, w]
    if residual is not None:
        in_specs.append(tile)
        operands.append(residual)
    if transpose_out:
        out_spec, out_shape = pl.BlockSpec((tn, tm), lambda j, i: (j, i)), (n_cols, s)
    else:
        out_spec, out_shape = tile, (s, n_cols)
    return pl.pallas_call(
        functools.partial(_matmul_kernel, col_blocks=tuple(col_blocks), tn=tn, n_scaled_blocks=n_scaled_blocks,
                          scale=scale, has_residual=residual is not None, transpose_out=transpose_out),
        grid=(len(col_blocks), s // tm),
        in_specs=in_specs,
        out_specs=out_spec,
        out_shape=jax.ShapeDtypeStruct(out_shape, out_dtype),
        scratch_shapes=_weight_stage_scratch(k, tn),
        compiler_params=_params("arbitrary", "arbitrary"),
        name=name,
    )(*operands)


def _attn_kernel(lq1_ref, lk1_ref, lq2_ref, lk2_ref, gsub_ref, qt_ref, k_ref, vt_ref, za_ref,
                 o_ref, acc1_ref, acc2_ref, sa_ref, sb_ref, *, tq):
    tk = tq // 2
    qi = pl.program_id(1)
    lam = (jnp.exp(jnp.sum(lq1_ref[...] * lk1_ref[...], axis=-1, keepdims=True))
           - jnp.exp(jnp.sum(lq2_ref[...] * lk2_ref[...], axis=-1, keepdims=True))
           + LAMBDA_INIT)
    acc_refs = (acc1_ref, acc2_ref)
    acc1_ref[...] = jnp.zeros_like(acc1_ref)
    acc2_ref[...] = jnp.zeros_like(acc2_ref)

    def qk(c, s_ref, q0=0):
        kc = k_ref[pl.ds(pl.multiple_of(c * tk, tk), tk), :]
        for comp in range(2):
            dims = slice(comp * HEAD_DIM, (comp + 1) * HEAD_DIM)
            s_ref[comp, :, q0:] = jnp.dot(kc[:, dims], qt_ref[dims, q0:], preferred_element_type=F32)

    def softmax_pv(c, s_ref, carry, masked, q0=0):
        vtc = vt_ref[:, pl.ds(pl.multiple_of(c * tk, tk), tk)]
        new = []
        for comp in range(2):
            m, l = carry[comp]
            s = s_ref[comp, :, q0:]
            if masked:
                key = c * tk + lax.broadcasted_iota(jnp.int32, s.shape, 0)
                query = qi * tq + q0 + lax.broadcasted_iota(jnp.int32, s.shape, 1)
                s = jnp.where(key <= query, s, -jnp.inf)
            m_new = jnp.maximum(m[:, q0:], jnp.max(s, axis=0, keepdims=True))
            alpha = jnp.exp2(m[:, q0:] - m_new)
            p = jnp.exp2(s - m_new)
            l_new = alpha * l[:, q0:] + jnp.sum(p, axis=0, keepdims=True)
            acc_ref = acc_refs[comp]
            acc_ref[:, q0:] = alpha * acc_ref[:, q0:] + jnp.dot(vtc, p.astype(BF16), preferred_element_type=F32)
            if q0:
                m_new = jnp.concatenate([m[:, :q0], m_new], axis=1)
                l_new = jnp.concatenate([l[:, :q0], l_new], axis=1)
            new.append((m_new, l_new))
        return tuple(new)

    def pair(j, carry):
        c0 = 2 * j
        qk(c0 + 1, sb_ref)
        carry = softmax_pv(c0, sa_ref, carry, masked=False)
        qk(c0 + 2, sa_ref)
        return softmax_pv(c0 + 1, sb_ref, carry, masked=False)

    neg = jnp.full((1, tq), -jnp.inf, F32)
    zero = jnp.zeros((1, tq), F32)
    qk(0, sa_ref)
    carry = lax.fori_loop(0, qi, pair, ((neg, zero), (neg, zero)))
    qk(2 * qi + 1, sb_ref, q0=tk)
    carry = softmax_pv(2 * qi, sa_ref, carry, masked=True)
    (_, l1), (_, l2) = softmax_pv(2 * qi + 1, sb_ref, carry, masked=True, q0=tk)

    ot = acc1_ref[...] / l1 - lam * (acc2_ref[...] / l2)
    ot = ot * lax.rsqrt(jnp.mean(ot * ot, axis=0, keepdims=True) + SUBLN_EPS)
    o = ot.T * gsub_ref[...] * (1.0 - LAMBDA_INIT)
    o_ref[...] = (o * _silu(za_ref[...])).astype(o_ref.dtype)


def _attention(qvt, k, rest, lq1, lk1, lq2, lk2, g_subln, tq=1024):
    s = k.shape[0]
    vec = pl.BlockSpec((1, HEAD_DIM), lambda h, i: (0, 0))
    score_buf = pltpu.VMEM((2, tq // 2, tq), F32)
    return pl.pallas_call(
        functools.partial(_attn_kernel, tq=tq),
        grid=(N_HEADS, s // tq),
        in_specs=[vec, vec, vec, vec,
                  pl.BlockSpec((1, V_DIM), lambda h, i: (0, 0)),
                  pl.BlockSpec((V_DIM, tq), lambda h, i: (h, i)),
                  pl.BlockSpec((s, V_DIM), lambda h, i: (0, h)),
                  pl.BlockSpec((V_DIM, s), lambda h, i: (N_HEADS + h, 0)),
                  pl.BlockSpec((tq, V_DIM), lambda h, i: (i, h))],
        out_specs=pl.BlockSpec((tq, V_DIM), lambda h, i: (i, h)),
        out_shape=jax.ShapeDtypeStruct((s, ATT_WIDTH), BF16),
        scratch_shapes=[pltpu.VMEM((V_DIM, tq), F32), pltpu.VMEM((V_DIM, tq), F32), score_buf, score_buf],
        compiler_params=_params("parallel", "arbitrary"),
        name="diff_attention",
    )(lq1.reshape(1, -1), lk1.reshape(1, -1), lq2.reshape(1, -1), lk2.reshape(1, -1),
      g_subln.reshape(1, -1), qvt, k, qvt, rest)


def _conv_kernel(ga_ref, gb_ref, ga_halo_ref, gb_halo_ref, zc_ref, w_ref, b_ref, lng_ref, lnb_ref,
                 o_ref, ext_ref, y_ref, *, tr, lane_chunk, row_chunk):
    i = pl.program_id(0)
    halo = ga_halo_ref[...] * jax.nn.sigmoid(gb_halo_ref[...])
    ext_ref[0:CONV_HALO, :] = jnp.where(i > 0, halo, 0.0)

    def glu(rows):
        ext_ref[CONV_HALO + rows.start:CONV_HALO + rows.stop, :] = ga_ref[rows, :] * jax.nn.sigmoid(gb_ref[rows, :])
    _for_row_chunks(tr, 16, glu)

    first_tap = CONV_HALO - (CONV_KERNEL - 1)
    slab_rows = row_chunk + CONV_HALO

    def lane_body(c, _):
        lanes = pl.ds(pl.multiple_of(c * lane_chunk, lane_chunk), lane_chunk)
        for r0 in range(0, tr, row_chunk):
            slab = ext_ref[r0:r0 + slab_rows, lanes]
            acc = jnp.broadcast_to(b_ref[:, lanes], (row_chunk, lane_chunk))
            for phase in range(8):
                shifted = slab if phase == 0 else pltpu.roll(slab, slab_rows - phase, 0)
                for j in range(CONV_KERNEL):
                    if (first_tap + j) % 8 == phase:
                        off = first_tap + j - phase
                        acc = acc + w_ref[j:j + 1, lanes] * shifted[off:off + row_chunk]
            y_ref[r0:r0 + row_chunk, lanes] = acc
        return 0

    lax.fori_loop(0, CONV_WIDTH // lane_chunk, lane_body, 0)

    def norm_gate(rows):
        y = y_ref[rows, :]
        yc = y - jnp.mean(y, axis=-1, keepdims=True)
        ln = yc * lax.rsqrt(jnp.mean(yc * yc, axis=-1, keepdims=True) + LN_EPS) * lng_ref[...] + lnb_ref[...]
        o_ref[rows, :] = (_silu(ln) * _silu(zc_ref[rows, :])).astype(o_ref.dtype)
    _for_row_chunks(tr, 16, norm_gate)


def _conv_branch(rest, conv_w, conv_b, ln_g, ln_b, tr=256):
    s = rest.shape[0]
    c = CONV_WIDTH
    halo_blocks = tr // CONV_HALO
    row = lambda col: pl.BlockSpec((tr, c), lambda i: (i, col))
    halo = lambda col: pl.BlockSpec((CONV_HALO, c), lambda i: (jnp.maximum(i * halo_blocks - 1, 0), col))
    vec = pl.BlockSpec((1, c), lambda i: (0, 0))
    return pl.pallas_call(
        functools.partial(_conv_kernel, tr=tr, lane_chunk=128, row_chunk=128),
        grid=(s // tr,),
        in_specs=[row(1), row(2), halo(1), halo(2), row(3),
                  pl.BlockSpec((CONV_KERNEL, c), lambda i: (0, 0)), vec, vec, vec],
        out_specs=pl.BlockSpec((tr, c), lambda i: (i, 0)),
        out_shape=jax.ShapeDtypeStruct((s, c), BF16),
        scratch_shapes=[pltpu.VMEM((CONV_HALO + tr, c), F32), pltpu.VMEM((tr, c), F32)],
        compiler_params=_params("parallel"),
        name="conv_branch",
    )(rest, rest, rest, rest, rest, conv_w, conv_b.reshape(1, c), ln_g.reshape(1, c), ln_b.reshape(1, c))


def _merge_kernel(aa_ref, ac_ref, wa_hbm_ref, wc_hbm_ref, ga_ref, gc_ref, o_ref,
                  waf_ref, wab_ref, sem_a, wcf_ref, wcb_ref, sem_c, *, tn):
    col_blocks = tuple(range(D_MODEL // tn))
    _stage_weight_block(wa_hbm_ref, waf_ref, wab_ref, sem_a, col_blocks, tn)
    _stage_weight_block(wc_hbm_ref, wcf_ref, wcb_ref, sem_c, col_blocks, tn)
    ya = jnp.dot(aa_ref[...], wab_ref[...], preferred_element_type=F32)
    yc = jnp.dot(ac_ref[...], wcb_ref[...], preferred_element_type=F32)
    m = jax.nn.sigmoid(ga_ref[...]) * ya + jax.nn.sigmoid(gc_ref[...]) * yc
    o_ref[...] = m.astype(o_ref.dtype)


def _merge(a_att, a_conv, w_att_out, w_conv_out, rest, tm=512, tn=1024):
    s, k = a_att.shape
    n = D_MODEL
    gate_a_block = (ATT_WIDTH + 3 * CONV_WIDTH) // tn
    gate_c_block = gate_a_block + D_MODEL // tn
    return pl.pallas_call(
        functools.partial(_merge_kernel, tn=tn),
        grid=(n // tn, s // tm),
        in_specs=[pl.BlockSpec((tm, k), lambda j, i: (i, 0)),
                  pl.BlockSpec((tm, k), lambda j, i: (i, 0)),
                  pl.BlockSpec(memory_space=pl.ANY),
                  pl.BlockSpec(memory_space=pl.ANY),
                  pl.BlockSpec((tm, tn), lambda j, i: (i, j + gate_a_block)),
                  pl.BlockSpec((tm, tn), lambda j, i: (i, j + gate_c_block))],
        out_specs=pl.BlockSpec((tm, tn), lambda j, i: (i, j)),
        out_shape=jax.ShapeDtypeStruct((s, n), BF16),
        scratch_shapes=_weight_stage_scratch(k, tn) + _weight_stage_scratch(k, tn),
        compiler_params=_params("arbitrary", "arbitrary"),
        name="gated_merge",
    )(a_att, a_conv, w_att_out, w_conv_out, rest, rest)


def _ple_kernel(x1_ref, wg_ref, p_ref, wp_ref, ggate_ref, gpost_ref, gfin_ref, o_ref, hg_ref, *, tm, tn):
    def gate_norm(rows):
        xr = x1_ref[rows, :]
        hg_ref[rows, :] = (xr * _rms_scale(xr, RMS_EPS) * ggate_ref[...]).astype(BF16)
    _for_row_chunks(tm, 16, gate_norm)

    o_ref[...] = jnp.dot(p_ref[...].astype(BF16), wp_ref[...], preferred_element_type=F32)

    def embed_norm(rows):
        er = o_ref[rows, :]
        o_ref[rows, :] = er * _rms_scale(er, RMS_EPS) * gpost_ref[...]
    _for_row_chunks(tm, 8, embed_norm)

    for c in range(D_MODEL // tn):
        cols = slice(c * tn, (c + 1) * tn)
        g = jax.nn.sigmoid(jnp.dot(hg_ref[...], wg_ref[:, cols], preferred_element_type=F32))
        o_ref[:, cols] = x1_ref[:, cols] + g * o_ref[:, cols]

    def final_norm(rows):
        xr = o_ref[rows, :]
        o_ref[rows, :] = xr * _rms_scale(xr, RMS_EPS) * gfin_ref[...]
    _for_row_chunks(tm, 8, final_norm)


def _ple(x1, w_ple_gate, p, w_ple, g_ple_gate, g_ple_post, g_final, tm=256, tn=512):
    s, n = x1.shape
    vec = pl.BlockSpec((1, n), lambda i: (0, 0))
    resident = pl.Buffered(1)
    return pl.pallas_call(
        functools.partial(_ple_kernel, tm=tm, tn=tn),
        grid=(s // tm,),
        in_specs=[pl.BlockSpec((tm, n), lambda i: (i, 0)),
                  pl.BlockSpec((n, n), lambda i: (0, 0), pipeline_mode=resident),
                  pl.BlockSpec((tm, PLE_DIM), lambda i: (i, 0)),
                  pl.BlockSpec((PLE_DIM, n), lambda i: (0, 0), pipeline_mode=resident),
                  vec, vec, vec],
        out_specs=pl.BlockSpec((tm, n), lambda i: (i, 0)),
        out_shape=jax.ShapeDtypeStruct((s, n), F32),
        scratch_shapes=[pltpu.VMEM((tm, n), BF16)],
        compiler_params=_params("parallel"),
        name="ple_final",
    )(x1, w_ple_gate, p, w_ple, g_ple_gate.reshape(1, n), g_ple_post.reshape(1, n), g_final.reshape(1, n))


def kernel(x, p, g_mix, w_in, lambda_q1, lambda_k1, lambda_q2, lambda_k2, g_subln, w_att_out, conv_w,
           conv_b, ln_g, ln_b, w_conv_out, w_out, g_ple_gate, w_ple_gate, w_ple, g_ple_post, g_final):
    b, s, d = x.shape
    assert (b, s, d) == (1, SEQ, D_MODEL) and p.shape[0] == 1 and w_in.shape[0] == 1
    x2d = x.reshape(s, d)

    h = _rmsnorm(x2d, g_mix[0])
    tn = 1024
    att_blocks = ATT_WIDTH // tn
    q_blocks = tuple(range(att_blocks))
    k_blocks = tuple(range(att_blocks, 2 * att_blocks))
    v_blocks = tuple(range(2 * att_blocks, 3 * att_blocks))
    rest_blocks = tuple(range(3 * att_blocks, w_in.shape[-1] // tn))
    qvt = _matmul(h, w_in[0], q_blocks + v_blocks, BF16, tm=1024, tn=tn, name="proj_qv_t",
                  n_scaled_blocks=att_blocks, scale=QK_LOG2_SCALE, transpose_out=True)
    k = _matmul(h, w_in[0], k_blocks, BF16, tm=1024, tn=tn, name="proj_k")
    rest = _matmul(h, w_in[0], rest_blocks, F32, tm=1024, tn=tn, name="proj_rest")

    a_att = _attention(qvt, k, rest, lambda_q1[0], lambda_k1[0], lambda_q2[0], lambda_k2[0], g_subln[0])
    a_conv = _conv_branch(rest, conv_w[0], conv_b[0], ln_g[0], ln_b[0])
    m = _merge(a_att, a_conv, w_att_out[0], w_conv_out[0], rest)
    x1 = _matmul(m, w_out[0], tuple(range(D_MODEL // tn)), F32, tm=512, tn=tn, name="out_proj", residual=x2d)
    out = _ple(x1, w_ple_gate[0].astype(BF16), p[0, 0], w_ple[0].astype(BF16),
               g_ple_gate[0], g_ple_post[0], g_final)
    return out.reshape(b, s, d)
```

```python
import functools
import math

import jax
import jax.numpy as jnp
from jax import lax
from jax.experimental import pallas as pl
from jax.experimental.pallas import tpu as pltpu

D_MODEL = 4096
SEQ = 8192
PLE_DIM = 256
N_HEADS = 8
HEAD_DIM = 128
V_DIM = 2 * HEAD_DIM
ATT_WIDTH = N_HEADS * V_DIM
CONV_WIDTH = D_MODEL // 2
CONV_KERNEL = 31

RMS_EPS = 1e-6
SUBLN_EPS = 1e-5
LN_EPS = 1e-5
LAMBDA_INIT = 0.8 - 0.6 * math.exp(-0.3 * 0)
QK_LOG2_SCALE = HEAD_DIM ** -0.5 * math.log2(math.e)

VMEM_LIMIT_BYTES = 56 * 1024 * 1024
CONV_HALO = 32

BF16 = jnp.bfloat16
F32 = jnp.float32


def _params(*semantics):
    return pltpu.CompilerParams(dimension_semantics=semantics, vmem_limit_bytes=VMEM_LIMIT_BYTES)


def _rms_scale(v, eps):
    return lax.rsqrt(jnp.mean(v * v, axis=-1, keepdims=True) + eps)


def _sigmoid(v):
    return 0.5 * jnp.tanh(0.5 * v) + 0.5


def _silu(v):
    return v * _sigmoid(v)


def _for_row_chunks(n_rows, chunk, body):
    for r in range(0, n_rows, chunk):
        body(slice(r, r + chunk))


def _norm_proj_kernel(x_ref, g_ref, w_hbm_ref, h_ref, o_ref, wf_ref, wb_ref, sem, *, tm, first_col, stage_cols):
    @pl.when(pl.program_id(0) == 0)
    def _():
        for c in range(0, wb_ref.shape[1], stage_cols):
            piece = pltpu.make_async_copy(w_hbm_ref.at[:, pl.ds(first_col + c, stage_cols)], wf_ref, sem)
            piece.start()
            piece.wait()
            wb_ref[:, c:c + stage_cols] = wf_ref[...].astype(BF16)

    def norm(rows):
        xr = x_ref[rows, :]
        h_ref[rows, :] = (xr * _rms_scale(xr, RMS_EPS) * g_ref[...]).astype(BF16)
    _for_row_chunks(tm, 16, norm)
    o_ref[...] = jnp.dot(h_ref[...], wb_ref[...], preferred_element_type=F32).astype(o_ref.dtype)


def _norm_proj(x, g, w, first_col, n_cols, tm=512, stage_cols=512):
    s, d = x.shape
    return pl.pallas_call(
        functools.partial(_norm_proj_kernel, tm=tm, first_col=first_col, stage_cols=stage_cols),
        grid=(s // tm,),
        in_specs=[pl.BlockSpec((tm, d), lambda i: (i, 0)),
                  pl.BlockSpec((1, d), lambda i: (0, 0)),
                  pl.BlockSpec(memory_space=pl.ANY)],
        out_specs=[pl.BlockSpec((tm, d), lambda i: (i, 0)),
                   pl.BlockSpec((tm, n_cols), lambda i: (i, 0))],
        out_shape=[jax.ShapeDtypeStruct((s, d), BF16), jax.ShapeDtypeStruct((s, n_cols), BF16)],
        scratch_shapes=[pltpu.VMEM((d, stage_cols), F32), pltpu.VMEM((d, n_cols), BF16),
                        pltpu.SemaphoreType.DMA(())],
        compiler_params=_params("arbitrary"),
        name="norm_proj_k",
    )(x, g.reshape(1, d), w)


def _weight_col_block(j, col_blocks):
    block = j + col_blocks[0]
    for pos in range(1, len(col_blocks)):
        gap = col_blocks[pos] - col_blocks[pos - 1] - 1
        if gap:
            block = block + jnp.where(j >= pos, gap, 0)
    return block


def _stage_weight_block(w_hbm_ref, wf_ref, wb_ref, sem, col_blocks, tn):
    j = pl.program_id(0)

    def weight_copy(step):
        col = pl.multiple_of(_weight_col_block(step, col_blocks) * tn, tn)
        return pltpu.make_async_copy(w_hbm_ref.at[:, pl.ds(col, tn)], wf_ref, sem)

    @pl.when(pl.program_id(1) == 0)
    def _():
        @pl.when(j == 0)
        def _():
            weight_copy(0).start()

        weight_copy(j).wait()
        wb_ref[...] = wf_ref[...].astype(BF16)

        @pl.when(j + 1 < pl.num_programs(0))
        def _():
            weight_copy(j + 1).start()


def _weight_stage_scratch(k, tn):
    return [pltpu.VMEM((k, tn), F32), pltpu.VMEM((k, tn), BF16), pltpu.SemaphoreType.DMA(())]


def _matmul_kernel(*refs, col_blocks, tn, n_scaled_blocks, scale, has_residual, transpose_out):
    if has_residual:
        a_ref, w_hbm_ref, x_ref, o_ref, wf_ref, wb_ref, sem = refs
    else:
        a_ref, w_hbm_ref, o_ref, wf_ref, wb_ref, sem = refs
    _stage_weight_block(w_hbm_ref, wf_ref, wb_ref, sem, col_blocks, tn)
    acc = jnp.dot(a_ref[...], wb_ref[...], preferred_element_type=F32)
    if n_scaled_blocks:
        acc = acc * jnp.where(pl.program_id(0) < n_scaled_blocks, scale, 1.0)
    if has_residual:
        acc = x_ref[...] + acc
    if transpose_out:
        acc = acc.T
    o_ref[...] = acc.astype(o_ref.dtype)


def _matmul(a, w, col_blocks, out_dtype, tm, tn, name, n_scaled_blocks=0, scale=1.0, residual=None,
            transpose_out=False):
    s, k = a.shape
    n_cols = len(col_blocks) * tn
    tile = pl.BlockSpec((tm, tn), lambda j, i: (i, j))
    in_specs = [pl.BlockSpec((tm, k), lambda j, i: (i, 0)), pl.BlockSpec(memory_space=pl.ANY)]
    operands = [a, w]
    if residual is not None:
        in_specs.append(tile)
        operands.append(residual)
    if transpose_out:
        out_spec, out_shape = pl.BlockSpec((tn, tm), lambda j, i: (j, i)), (n_cols, s)
    else:
        out_spec, out_shape = tile, (s, n_cols)
    return pl.pallas_call(
        functools.partial(_matmul_kernel, col_blocks=tuple(col_blocks), tn=tn, n_scaled_blocks=n_scaled_blocks,
                          scale=scale, has_residual=residual is not None, transpose_out=transpose_out),
        grid=(len(col_blocks), s // tm),
        in_specs=in_specs,
        out_specs=out_spec,
        out_shape=jax.ShapeDtypeStruct(out_shape, out_dtype),
        scratch_shapes=_weight_stage_scratch(k, tn),
        compiler_params=_params("arbitrary", "arbitrary"),
        name=name,
    )(*operands)


def _attn_kernel(lq1_ref, lk1_ref, lq2_ref, lk2_ref, gsub_ref, qt_ref, qt_next_ref, k_ref, vt_ref, za_ref,
                 o_ref, acc1_ref, acc2_ref, sa_ref, sb_ref, *, tq):
    tk = tq // 2
    qi = pl.program_id(1)
    lam = (jnp.exp(jnp.sum(lq1_ref[...] * lk1_ref[...], axis=-1, keepdims=True))
           - jnp.exp(jnp.sum(lq2_ref[...] * lk2_ref[...], axis=-1, keepdims=True))
           + LAMBDA_INIT)
    acc_refs = (acc1_ref, acc2_ref)
    acc1_ref[...] = jnp.zeros_like(acc1_ref)
    acc2_ref[...] = jnp.zeros_like(acc2_ref)

    def qk(c, s_ref, q0=0, queries_ref=qt_ref):
        kc = k_ref[pl.ds(pl.multiple_of(c * tk, tk), tk), :]
        for comp in range(2):
            dims = slice(comp * HEAD_DIM, (comp + 1) * HEAD_DIM)
            s_ref[comp, :, q0:] = jnp.dot(kc[:, dims], queries_ref[dims, q0:], preferred_element_type=F32)

    def softmax_pv(c, s_ref, carry, masked, q0=0):
        vtc = vt_ref[:, pl.ds(pl.multiple_of(c * tk, tk), tk)]
        new = []
        for comp in range(2):
            m, l = carry[comp]
            s = s_ref[comp, :, q0:]
            if masked:
                key = c * tk + lax.broadcasted_iota(jnp.int32, s.shape, 0)
                query = qi * tq + q0 + lax.broadcasted_iota(jnp.int32, s.shape, 1)
                s = jnp.where(key <= query, s, -jnp.inf)
            m_new = jnp.maximum(m[:, q0:], jnp.max(s, axis=0, keepdims=True))
            alpha = jnp.exp2(m[:, q0:] - m_new)
            p = jnp.exp2(s - m_new)
            l_new = alpha * l[:, q0:] + jnp.sum(p, axis=0, keepdims=True)
            acc_ref = acc_refs[comp]
            acc_ref[:, q0:] = alpha * acc_ref[:, q0:] + jnp.dot(vtc, p.astype(BF16), preferred_element_type=F32)
            if q0:
                m_new = jnp.concatenate([m[:, :q0], m_new], axis=1)
                l_new = jnp.concatenate([l[:, :q0], l_new], axis=1)
            new.append((m_new, l_new))
        return tuple(new)

    def pair(j, carry):
        c0 = 2 * j
        qk(c0 + 1, sb_ref)
        carry = softmax_pv(c0, sa_ref, carry, masked=False)
        qk(c0 + 2, sa_ref)
        return softmax_pv(c0 + 1, sb_ref, carry, masked=False)

    neg = jnp.full((1, tq), -jnp.inf, F32)
    zero = jnp.zeros((1, tq), F32)

    @pl.when(qi == 0)
    def _():
        qk(0, sa_ref)

    carry = lax.fori_loop(0, qi, pair, ((neg, zero), (neg, zero)))
    qk(2 * qi + 1, sb_ref, q0=tk)
    carry = softmax_pv(2 * qi, sa_ref, carry, masked=True)
    (_, l1), (_, l2) = softmax_pv(2 * qi + 1, sb_ref, carry, masked=True, q0=tk)

    qk(0, sa_ref, queries_ref=qt_next_ref)

    ot = acc1_ref[...] / l1 - lam * (acc2_ref[...] / l2)
    ot = ot * lax.rsqrt(jnp.mean(ot * ot, axis=0, keepdims=True) + SUBLN_EPS)
    o = ot.T * gsub_ref[...] * (1.0 - LAMBDA_INIT)
    o_ref[...] = (o * _silu(za_ref[...])).astype(o_ref.dtype)


def _attention(qvt, k, rest, lq1, lk1, lq2, lk2, g_subln, tq=1024):
    s = k.shape[0]
    n_tiles = s // tq
    vec = pl.BlockSpec((1, HEAD_DIM), lambda h, i: (0, 0))
    score_buf = pltpu.VMEM((2, tq // 2, tq), F32)
    return pl.pallas_call(
        functools.partial(_attn_kernel, tq=tq),
        grid=(N_HEADS, n_tiles),
        in_specs=[vec, vec, vec, vec,
                  pl.BlockSpec((1, V_DIM), lambda h, i: (0, 0)),
                  pl.BlockSpec((V_DIM, tq), lambda h, i: (h, i)),
                  pl.BlockSpec((V_DIM, tq), lambda h, i: (h, jnp.minimum(i + 1, n_tiles - 1))),
                  pl.BlockSpec((s, V_DIM), lambda h, i: (0, h)),
                  pl.BlockSpec((V_DIM, s), lambda h, i: (N_HEADS + h, 0)),
                  pl.BlockSpec((tq, V_DIM), lambda h, i: (i, h))],
        out_specs=pl.BlockSpec((tq, V_DIM), lambda h, i: (i, h)),
        out_shape=jax.ShapeDtypeStruct((s, ATT_WIDTH), BF16),
        scratch_shapes=[pltpu.VMEM((V_DIM, tq), F32), pltpu.VMEM((V_DIM, tq), F32), score_buf, score_buf],
        compiler_params=_params("parallel", "arbitrary"),
        name="diff_attention",
    )(lq1.reshape(1, -1), lk1.reshape(1, -1), lq2.reshape(1, -1), lk2.reshape(1, -1),
      g_subln.reshape(1, -1), qvt, qvt, k, qvt, rest)


def _conv_kernel(ga_ref, gb_ref, ga_halo_ref, gb_halo_ref, zc_ref, w_ref, b_ref, lng_ref, lnb_ref,
                 o_ref, ext_ref, y_ref, *, tr, lane_chunk, row_chunk):
    i = pl.program_id(0)
    halo = ga_halo_ref[...] * _sigmoid(gb_halo_ref[...])
    ext_ref[0:CONV_HALO, :] = jnp.where(i > 0, halo, 0.0)

    def glu(rows):
        ext_ref[CONV_HALO + rows.start:CONV_HALO + rows.stop, :] = ga_ref[rows, :] * _sigmoid(gb_ref[rows, :])
    _for_row_chunks(tr, 16, glu)

    first_tap = CONV_HALO - (CONV_KERNEL - 1)
    slab_rows = row_chunk + CONV_HALO

    def lane_body(c, _):
        lanes = pl.ds(pl.multiple_of(c * lane_chunk, lane_chunk), lane_chunk)
        for r0 in range(0, tr, row_chunk):
            slab = ext_ref[r0:r0 + slab_rows, lanes]
            acc = jnp.broadcast_to(b_ref[:, lanes], (row_chunk, lane_chunk))
            for phase in range(8):
                shifted = slab if phase == 0 else pltpu.roll(slab, slab_rows - phase, 0)
                for j in range(CONV_KERNEL):
                    if (first_tap + j) % 8 == phase:
                        off = first_tap + j - phase
                        acc = acc + w_ref[j:j + 1, lanes] * shifted[off:off + row_chunk]
            y_ref[r0:r0 + row_chunk, lanes] = acc
        return 0

    lax.fori_loop(0, CONV_WIDTH // lane_chunk, lane_body, 0)

    def norm_gate(rows):
        y = y_ref[rows, :]
        yc = y - jnp.mean(y, axis=-1, keepdims=True)
        ln = yc * lax.rsqrt(jnp.mean(yc * yc, axis=-1, keepdims=True) + LN_EPS) * lng_ref[...] + lnb_ref[...]
        o_ref[rows, :] = (_silu(ln) * _silu(zc_ref[rows, :])).astype(o_ref.dtype)
    _for_row_chunks(tr, 16, norm_gate)


def _conv_branch(rest, conv_w, conv_b, ln_g, ln_b, tr=256):
    s = rest.shape[0]
    c = CONV_WIDTH
    halo_blocks = tr // CONV_HALO
    row = lambda col: pl.BlockSpec((tr, c), lambda i: (i, col))
    halo = lambda col: pl.BlockSpec((CONV_HALO, c), lambda i: (jnp.maximum(i * halo_blocks - 1, 0), col))
    vec = pl.BlockSpec((1, c), lambda i: (0, 0))
    return pl.pallas_call(
        functools.partial(_conv_kernel, tr=tr, lane_chunk=128, row_chunk=128),
        grid=(s // tr,),
        in_specs=[row(1), row(2), halo(1), halo(2), row(3),
                  pl.BlockSpec((CONV_KERNEL, c), lambda i: (0, 0)), vec, vec, vec],
        out_specs=pl.BlockSpec((tr, c), lambda i: (i, 0)),
        out_shape=jax.ShapeDtypeStruct((s, c), BF16),
        scratch_shapes=[pltpu.VMEM((CONV_HALO + tr, c), F32), pltpu.VMEM((tr, c), F32)],
        compiler_params=_params("parallel"),
        name="conv_branch",
    )(rest, rest, rest, rest, rest, conv_w, conv_b.reshape(1, c), ln_g.reshape(1, c), ln_b.reshape(1, c))


def _merge_kernel(aa_ref, ac_ref, wa_hbm_ref, wc_hbm_ref, ga_ref, gc_ref, o_ref,
                  waf_ref, wab_ref, sem_a, wcf_ref, wcb_ref, sem_c, *, tn):
    col_blocks = tuple(range(D_MODEL // tn))
    _stage_weight_block(wa_hbm_ref, waf_ref, wab_ref, sem_a, col_blocks, tn)
    _stage_weight_block(wc_hbm_ref, wcf_ref, wcb_ref, sem_c, col_blocks, tn)
    ya = jnp.dot(aa_ref[...], wab_ref[...], preferred_element_type=F32)
    yc = jnp.dot(ac_ref[...], wcb_ref[...], preferred_element_type=F32)
    m = _sigmoid(ga_ref[...]) * ya + _sigmoid(gc_ref[...]) * yc
    o_ref[...] = m.astype(o_ref.dtype)


def _merge(a_att, a_conv, w_att_out, w_conv_out, rest, tm=512, tn=1024):
    s, k = a_att.shape
    n = D_MODEL
    gate_a_block = (ATT_WIDTH + 3 * CONV_WIDTH) // tn
    gate_c_block = gate_a_block + D_MODEL // tn
    return pl.pallas_call(
        functools.partial(_merge_kernel, tn=tn),
        grid=(n // tn, s // tm),
        in_specs=[pl.BlockSpec((tm, k), lambda j, i: (i, 0)),
                  pl.BlockSpec((tm, k), lambda j, i: (i, 0)),
                  pl.BlockSpec(memory_space=pl.ANY),
                  pl.BlockSpec(memory_space=pl.ANY),
                  pl.BlockSpec((tm, tn), lambda j, i: (i, j + gate_a_block)),
                  pl.BlockSpec((tm, tn), lambda j, i: (i, j + gate_c_block))],
        out_specs=pl.BlockSpec((tm, tn), lambda j, i: (i, j)),
        out_shape=jax.ShapeDtypeStruct((s, n), BF16),
        scratch_shapes=_weight_stage_scratch(k, tn) + _weight_stage_scratch(k, tn),
        compiler_params=_params("arbitrary", "arbitrary"),
        name="gated_merge",
    )(a_att, a_conv, w_att_out, w_conv_out, rest, rest)


def _ple_kernel(x1_ref, wg_ref, p_ref, wp_ref, ggate_ref, gpost_ref, gfin_ref, o_ref, hg_ref, *, tm, tn):
    def gate_norm(rows):
        xr = x1_ref[rows, :]
        hg_ref[rows, :] = (xr * _rms_scale(xr, RMS_EPS) * ggate_ref[...]).astype(BF16)
    _for_row_chunks(tm, 16, gate_norm)

    o_ref[...] = jnp.dot(p_ref[...].astype(BF16), wp_ref[...], preferred_element_type=F32)

    def embed_norm(rows):
        er = o_ref[rows, :]
        o_ref[rows, :] = er * _rms_scale(er, RMS_EPS) * gpost_ref[...]
    _for_row_chunks(tm, 8, embed_norm)

    for c in range(D_MODEL // tn):
        cols = slice(c * tn, (c + 1) * tn)
        g = _sigmoid(jnp.dot(hg_ref[...], wg_ref[:, cols], preferred_element_type=F32))
        o_ref[:, cols] = x1_ref[:, cols] + g * o_ref[:, cols]

    def final_norm(rows):
        xr = o_ref[rows, :]
        o_ref[rows, :] = xr * _rms_scale(xr, RMS_EPS) * gfin_ref[...]
    _for_row_chunks(tm, 8, final_norm)


def _ple(x1, w_ple_gate, p, w_ple, g_ple_gate, g_ple_post, g_final, tm=256, tn=512):
    s, n = x1.shape
    vec = pl.BlockSpec((1, n), lambda i: (0, 0))
    resident = pl.Buffered(1)
    return pl.pallas_call(
        functools.partial(_ple_kernel, tm=tm, tn=tn),
        grid=(s // tm,),
        in_specs=[pl.BlockSpec((tm, n), lambda i: (i, 0)),
                  pl.BlockSpec((n, n), lambda i: (0, 0), pipeline_mode=resident),
                  pl.BlockSpec((tm, PLE_DIM), lambda i: (i, 0)),
                  pl.BlockSpec((PLE_DIM, n), lambda i: (0, 0), pipeline_mode=resident),
                  vec, vec, vec],
        out_specs=pl.BlockSpec((tm, n), lambda i: (i, 0)),
        out_shape=jax.ShapeDtypeStruct((s, n), F32),
        scratch_shapes=[pltpu.VMEM((tm, n), BF16)],
        compiler_params=_params("parallel"),
        name="ple_final",
    )(x1, w_ple_gate, p, w_ple, g_ple_gate.reshape(1, n), g_ple_post.reshape(1, n), g_final.reshape(1, n))


def kernel(x, p, g_mix, w_in, lambda_q1, lambda_k1, lambda_q2, lambda_k2, g_subln, w_att_out, conv_w,
           conv_b, ln_g, ln_b, w_conv_out, w_out, g_ple_gate, w_ple_gate, w_ple, g_ple_post, g_final):
    b, s, d = x.shape
    assert (b, s, d) == (1, SEQ, D_MODEL) and p.shape[0] == 1 and w_in.shape[0] == 1
    x2d = x.reshape(s, d)

    h, k = _norm_proj(x2d, g_mix[0], w_in[0], first_col=ATT_WIDTH, n_cols=ATT_WIDTH)
    tn = 1024
    att_blocks = ATT_WIDTH // tn
    q_blocks = tuple(range(att_blocks))
    v_blocks = tuple(range(2 * att_blocks, 3 * att_blocks))
    rest_blocks = tuple(range(3 * att_blocks, w_in.shape[-1] // tn))
    qvt = _matmul(h, w_in[0], q_blocks + v_blocks, BF16, tm=1024, tn=tn, name="proj_qv_t",
                  n_scaled_blocks=att_blocks, scale=QK_LOG2_SCALE, transpose_out=True)
    rest = _matmul(h, w_in[0], rest_blocks, F32, tm=1024, tn=tn, name="proj_rest")

    a_att = _attention(qvt, k, rest, lambda_q1[0], lambda_k1[0], lambda_q2[0], lambda_k2[0], g_subln[0])
    a_conv = _conv_branch(rest, conv_w[0], conv_b[0], ln_g[0], ln_b[0])
    m = _merge(a_att, a_conv, w_att_out[0], w_conv_out[0], rest)
    x1 = _matmul(m, w_out[0], tuple(range(D_MODEL // tn)), F32, tm=512, tn=tn, name="out_proj", residual=x2d)
    out = _ple(x1, w_ple_gate[0].astype(BF16), p[0, 0], w_ple[0].astype(BF16),
               g_ple_gate[0], g_ple_post[0], g_final)
    return out.reshape(b, s, d)
```

```python
import functools
import math

import jax
import jax.numpy as jnp
from jax import lax
from jax.experimental import pallas as pl
from jax.experimental.pallas import tpu as pltpu

D_MODEL = 4096
SEQ = 8192
PLE_DIM = 256
N_HEADS = 8
HEAD_DIM = 128
V_DIM = 2 * HEAD_DIM
ATT_WIDTH = N_HEADS * V_DIM
CONV_WIDTH = D_MODEL // 2
CONV_KERNEL = 31

RMS_EPS = 1e-6
SUBLN_EPS = 1e-5
LN_EPS = 1e-5
LAMBDA_INIT = 0.8 - 0.6 * math.exp(-0.3 * 0)
QK_LOG2_SCALE = HEAD_DIM ** -0.5 * math.log2(math.e)

VMEM_LIMIT_BYTES = 56 * 1024 * 1024
CONV_HALO = 32

BF16 = jnp.bfloat16
F32 = jnp.float32


def _params(*semantics):
    return pltpu.CompilerParams(dimension_semantics=semantics, vmem_limit_bytes=VMEM_LIMIT_BYTES)


def _rms_scale(v, eps):
    return lax.rsqrt(jnp.mean(v * v, axis=-1, keepdims=True) + eps)


def _sigmoid(v):
    return 0.5 * jnp.tanh(0.5 * v) + 0.5


def _silu(v):
    return v * _sigmoid(v)


def _for_row_chunks(n_rows, chunk, body):
    for r in range(0, n_rows, chunk):
        body(slice(r, r + chunk))


def _norm_proj_kernel(x_ref, g_ref, w_hbm_ref, h_ref, o_ref, wf_ref, wb_ref, sem, *, tm, first_col, stage_cols):
    @pl.when(pl.program_id(0) == 0)
    def _():
        for c in range(0, wb_ref.shape[1], stage_cols):
            piece = pltpu.make_async_copy(w_hbm_ref.at[:, pl.ds(first_col + c, stage_cols)], wf_ref, sem)
            piece.start()
            piece.wait()
            wb_ref[:, c:c + stage_cols] = wf_ref[...].astype(BF16)

    def norm(rows):
        xr = x_ref[rows, :]
        h_ref[rows, :] = (xr * _rms_scale(xr, RMS_EPS) * g_ref[...]).astype(BF16)
    _for_row_chunks(tm, 16, norm)
    o_ref[...] = jnp.dot(h_ref[...], wb_ref[...], preferred_element_type=F32).astype(o_ref.dtype)


def _norm_proj(x, g, w, first_col, n_cols, tm=512, stage_cols=512):
    s, d = x.shape
    return pl.pallas_call(
        functools.partial(_norm_proj_kernel, tm=tm, first_col=first_col, stage_cols=stage_cols),
        grid=(s // tm,),
        in_specs=[pl.BlockSpec((tm, d), lambda i: (i, 0)),
                  pl.BlockSpec((1, d), lambda i: (0, 0)),
                  pl.BlockSpec(memory_space=pl.ANY)],
        out_specs=[pl.BlockSpec((tm, d), lambda i: (i, 0)),
                   pl.BlockSpec((tm, n_cols), lambda i: (i, 0))],
        out_shape=[jax.ShapeDtypeStruct((s, d), BF16), jax.ShapeDtypeStruct((s, n_cols), BF16)],
        scratch_shapes=[pltpu.VMEM((d, stage_cols), F32), pltpu.VMEM((d, n_cols), BF16),
                        pltpu.SemaphoreType.DMA(())],
        compiler_params=_params("arbitrary"),
        name="norm_proj_k",
    )(x, g.reshape(1, d), w)


def _weight_col_block(j, col_blocks):
    block = j + col_blocks[0]
    for pos in range(1, len(col_blocks)):
        gap = col_blocks[pos] - col_blocks[pos - 1] - 1
        if gap:
            block = block + jnp.where(j >= pos, gap, 0)
    return block


def _stage_weight_block(w_hbm_ref, wf_ref, wb_ref, sem, col_blocks, tn):
    j = pl.program_id(0)

    def weight_copy(step):
        col = pl.multiple_of(_weight_col_block(step, col_blocks) * tn, tn)
        return pltpu.make_async_copy(w_hbm_ref.at[:, pl.ds(col, tn)], wf_ref, sem)

    @pl.when(pl.program_id(1) == 0)
    def _():
        @pl.when(j == 0)
        def _():
            weight_copy(0).start()

        weight_copy(j).wait()
        wb_ref[...] = wf_ref[...].astype(BF16)

        @pl.when(j + 1 < pl.num_programs(0))
        def _():
            weight_copy(j + 1).start()


def _weight_stage_scratch(k, tn):
    return [pltpu.VMEM((k, tn), F32), pltpu.VMEM((k, tn), BF16), pltpu.SemaphoreType.DMA(())]


def _matmul_kernel(*refs, col_blocks, tn, n_scaled_blocks, scale, has_residual, transpose_out):
    if has_residual:
        a_ref, w_hbm_ref, x_ref, o_ref, wf_ref, wb_ref, sem = refs
    else:
        a_ref, w_hbm_ref, o_ref, wf_ref, wb_ref, sem = refs
    _stage_weight_block(w_hbm_ref, wf_ref, wb_ref, sem, col_blocks, tn)
    acc = jnp.dot(a_ref[...], wb_ref[...], preferred_element_type=F32)
    if n_scaled_blocks:
        acc = acc * jnp.where(pl.program_id(0) < n_scaled_blocks, scale, 1.0)
    if has_residual:
        acc = x_ref[...] + acc
    if transpose_out:
        acc = acc.T
    o_ref[...] = acc.astype(o_ref.dtype)


def _matmul(a, w, col_blocks, out_dtype, tm, tn, name, n_scaled_blocks=0, scale=1.0, residual=None,
            transpose_out=False):
    s, k = a.shape
    n_cols = len(col_blocks) * tn
    tile = pl.BlockSpec((tm, tn), lambda j, i: (i, j))
    in_specs = [pl.BlockSpec((tm, k), lambda j, i: (i, 0)), pl.BlockSpec(memory_space=pl.ANY)]
    operands = [a, w]
    if residual is not None:
        in_specs.append(tile)
        operands.append(residual)
    if transpose_out:
        out_spec, out_shape = pl.BlockSpec((tn, tm), lambda j, i: (j, i)), (n_cols, s)
    else:
        out_spec, out_shape = tile, (s, n_cols)
    return pl.pallas_call(
        functools.partial(_matmul_kernel, col_blocks=tuple(col_blocks), tn=tn, n_scaled_blocks=n_scaled_blocks,
                          scale=scale, has_residual=residual is not None, transpose_out=transpose_out),
        grid=(len(col_blocks), s // tm),
        in_specs=in_specs,
        out_specs=out_spec,
        out_shape=jax.ShapeDtypeStruct(out_shape, out_dtype),
        scratch_shapes=_weight_stage_scratch(k, tn),
        compiler_params=_params("arbitrary", "arbitrary"),
        name=name,
    )(*operands)


def _attn_kernel(lq1_ref, lk1_ref, lq2_ref, lk2_ref, gsub_ref, qt_ref, qt_next_ref, k_ref, vt_ref, za_ref,
                 o_ref, acc1_ref, acc2_ref, sa_ref, sb_ref, ma_ref, mb_ref, *, tq):
    tk = tq // 2
    qi = pl.program_id(1)
    lam = (jnp.exp(jnp.sum(lq1_ref[...] * lk1_ref[...], axis=-1, keepdims=True))
           - jnp.exp(jnp.sum(lq2_ref[...] * lk2_ref[...], axis=-1, keepdims=True))
           + LAMBDA_INIT)
    acc_refs = (acc1_ref, acc2_ref)
    acc1_ref[...] = jnp.zeros_like(acc1_ref)
    acc2_ref[...] = jnp.zeros_like(acc2_ref)

    def qk(c, bufs, q0=0, queries_ref=qt_ref, with_max=True):
        s_ref, smax_ref = bufs
        kc = k_ref[pl.ds(pl.multiple_of(c * tk, tk), tk), :]
        for comp in range(2):
            dims = slice(comp * HEAD_DIM, (comp + 1) * HEAD_DIM)
            s = jnp.dot(kc[:, dims], queries_ref[dims, q0:], preferred_element_type=F32)
            s_ref[comp, :, q0:] = s
            if with_max:
                smax_ref[comp, :, q0:] = jnp.max(s, axis=0, keepdims=True)

    def softmax_pv(c, bufs, carry, masked, q0=0):
        s_ref, smax_ref = bufs
        vtc = vt_ref[:, pl.ds(pl.multiple_of(c * tk, tk), tk)]
        new = []
        for comp in range(2):
            m, l = carry[comp]
            s = s_ref[comp, :, q0:]
            if masked:
                key = c * tk + lax.broadcasted_iota(jnp.int32, s.shape, 0)
                query = qi * tq + q0 + lax.broadcasted_iota(jnp.int32, s.shape, 1)
                s = jnp.where(key <= query, s, -jnp.inf)
                chunk_max = jnp.max(s, axis=0, keepdims=True)
            else:
                chunk_max = smax_ref[comp, :, q0:]
            m_new = jnp.maximum(m[:, q0:], chunk_max)
            alpha = jnp.exp2(m[:, q0:] - m_new)
            p = jnp.exp2(s - m_new)
            l_new = alpha * l[:, q0:] + jnp.sum(p, axis=0, keepdims=True)
            acc_ref = acc_refs[comp]
            acc_ref[:, q0:] = alpha * acc_ref[:, q0:] + jnp.dot(vtc, p.astype(BF16), preferred_element_type=F32)
            if q0:
                m_new = jnp.concatenate([m[:, :q0], m_new], axis=1)
                l_new = jnp.concatenate([l[:, :q0], l_new], axis=1)
            new.append((m_new, l_new))
        return tuple(new)

    buf_a, buf_b = (sa_ref, ma_ref), (sb_ref, mb_ref)

    def pair(j, carry):
        c0 = 2 * j
        qk(c0 + 1, buf_b)
        carry = softmax_pv(c0, buf_a, carry, masked=False)
        qk(c0 + 2, buf_a)
        return softmax_pv(c0 + 1, buf_b, carry, masked=False)

    neg = jnp.full((1, tq), -jnp.inf, F32)
    zero = jnp.zeros((1, tq), F32)

    @pl.when(qi == 0)
    def _():
        qk(0, buf_a, with_max=False)

    carry = lax.fori_loop(0, qi, pair, ((neg, zero), (neg, zero)))
    qk(2 * qi + 1, buf_b, q0=tk, with_max=False)
    carry = softmax_pv(2 * qi, buf_a, carry, masked=True)
    (_, l1), (_, l2) = softmax_pv(2 * qi + 1, buf_b, carry, masked=True, q0=tk)

    qk(0, buf_a, queries_ref=qt_next_ref)

    ot = acc1_ref[...] * (1.0 / l1) - acc2_ref[...] * (lam / l2)
    ot = ot * lax.rsqrt(jnp.mean(ot * ot, axis=0, keepdims=True) + SUBLN_EPS)
    o = ot.T * gsub_ref[...] * (1.0 - LAMBDA_INIT)
    o_ref[...] = (o * _silu(za_ref[...])).astype(o_ref.dtype)


def _attention(qvt, k, rest, lq1, lk1, lq2, lk2, g_subln, tq=1024):
    s = k.shape[0]
    n_tiles = s // tq
    vec = pl.BlockSpec((1, HEAD_DIM), lambda h, i: (0, 0))
    score_buf = pltpu.VMEM((2, tq // 2, tq), F32)
    score_max = pltpu.VMEM((2, 1, tq), F32)
    return pl.pallas_call(
        functools.partial(_attn_kernel, tq=tq),
        grid=(N_HEADS, n_tiles),
        in_specs=[vec, vec, vec, vec,
                  pl.BlockSpec((1, V_DIM), lambda h, i: (0, 0)),
                  pl.BlockSpec((V_DIM, tq), lambda h, i: (h, i)),
                  pl.BlockSpec((V_DIM, tq), lambda h, i: (h, jnp.minimum(i + 1, n_tiles - 1))),
                  pl.BlockSpec((s, V_DIM), lambda h, i: (0, h)),
                  pl.BlockSpec((V_DIM, s), lambda h, i: (N_HEADS + h, 0)),
                  pl.BlockSpec((tq, V_DIM), lambda h, i: (i, h))],
        out_specs=pl.BlockSpec((tq, V_DIM), lambda h, i: (i, h)),
        out_shape=jax.ShapeDtypeStruct((s, ATT_WIDTH), BF16),
        scratch_shapes=[pltpu.VMEM((V_DIM, tq), F32), pltpu.VMEM((V_DIM, tq), F32), score_buf, score_buf,
                        score_max, score_max],
        compiler_params=_params("parallel", "arbitrary"),
        name="diff_attention",
    )(lq1.reshape(1, -1), lk1.reshape(1, -1), lq2.reshape(1, -1), lk2.reshape(1, -1),
      g_subln.reshape(1, -1), qvt, qvt, k, qvt, rest)


def _conv_kernel(ga_ref, gb_ref, ga_halo_ref, gb_halo_ref, zc_ref, w_ref, b_ref, lng_ref, lnb_ref,
                 o_ref, ext_ref, y_ref, *, tr, lane_chunk, row_chunk):
    i = pl.program_id(0)
    halo = ga_halo_ref[...] * _sigmoid(gb_halo_ref[...])
    ext_ref[0:CONV_HALO, :] = jnp.where(i > 0, halo, 0.0)

    def glu(rows):
        ext_ref[CONV_HALO + rows.start:CONV_HALO + rows.stop, :] = ga_ref[rows, :] * _sigmoid(gb_ref[rows, :])
    _for_row_chunks(tr, 16, glu)

    first_tap = CONV_HALO - (CONV_KERNEL - 1)
    slab_rows = row_chunk + CONV_HALO

    def lane_body(c, _):
        lanes = pl.ds(pl.multiple_of(c * lane_chunk, lane_chunk), lane_chunk)
        for r0 in range(0, tr, row_chunk):
            slab = ext_ref[r0:r0 + slab_rows, lanes]
            acc = jnp.broadcast_to(b_ref[:, lanes], (row_chunk, lane_chunk))
            for phase in range(8):
                shifted = slab if phase == 0 else pltpu.roll(slab, slab_rows - phase, 0)
                for j in range(CONV_KERNEL):
                    if (first_tap + j) % 8 == phase:
                        off = first_tap + j - phase
                        acc = acc + w_ref[j:j + 1, lanes] * shifted[off:off + row_chunk]
            y_ref[r0:r0 + row_chunk, lanes] = acc
        return 0

    lax.fori_loop(0, CONV_WIDTH // lane_chunk, lane_body, 0)

    def norm_gate(rows):
        y = y_ref[rows, :]
        yc = y - jnp.mean(y, axis=-1, keepdims=True)
        ln = yc * lax.rsqrt(jnp.mean(yc * yc, axis=-1, keepdims=True) + LN_EPS) * lng_ref[...] + lnb_ref[...]
        o_ref[rows, :] = (_silu(ln) * _silu(zc_ref[rows, :])).astype(o_ref.dtype)
    _for_row_chunks(tr, 16, norm_gate)


def _conv_branch(rest, conv_w, conv_b, ln_g, ln_b, tr=256):
    s = rest.shape[0]
    c = CONV_WIDTH
    halo_blocks = tr // CONV_HALO
    row = lambda col: pl.BlockSpec((tr, c), lambda i: (i, col))
    halo = lambda col: pl.BlockSpec((CONV_HALO, c), lambda i: (jnp.maximum(i * halo_blocks - 1, 0), col))
    vec = pl.BlockSpec((1, c), lambda i: (0, 0))
    return pl.pallas_call(
        functools.partial(_conv_kernel, tr=tr, lane_chunk=128, row_chunk=128),
        grid=(s // tr,),
        in_specs=[row(1), row(2), halo(1), halo(2), row(3),
                  pl.BlockSpec((CONV_KERNEL, c), lambda i: (0, 0)), vec, vec, vec],
        out_specs=pl.BlockSpec((tr, c), lambda i: (i, 0)),
        out_shape=jax.ShapeDtypeStruct((s, c), BF16),
        scratch_shapes=[pltpu.VMEM((CONV_HALO + tr, c), F32), pltpu.VMEM((tr, c), F32)],
        compiler_params=_params("parallel"),
        name="conv_branch",
    )(rest, rest, rest, rest, rest, conv_w, conv_b.reshape(1, c), ln_g.reshape(1, c), ln_b.reshape(1, c))


def _merge_kernel(aa_ref, ac_ref, wa_hbm_ref, wc_hbm_ref, ga_ref, gc_ref, o_ref,
                  waf_ref, wab_ref, sem_a, wcf_ref, wcb_ref, sem_c, *, tn):
    col_blocks = tuple(range(D_MODEL // tn))
    _stage_weight_block(wa_hbm_ref, waf_ref, wab_ref, sem_a, col_blocks, tn)
    _stage_weight_block(wc_hbm_ref, wcf_ref, wcb_ref, sem_c, col_blocks, tn)
    ya = jnp.dot(aa_ref[...], wab_ref[...], preferred_element_type=F32)
    yc = jnp.dot(ac_ref[...], wcb_ref[...], preferred_element_type=F32)
    m = _sigmoid(ga_ref[...]) * ya + _sigmoid(gc_ref[...]) * yc
    o_ref[...] = m.astype(o_ref.dtype)


def _merge(a_att, a_conv, w_att_out, w_conv_out, rest, tm=512, tn=1024):
    s, k = a_att.shape
    n = D_MODEL
    gate_a_block = (ATT_WIDTH + 3 * CONV_WIDTH) // tn
    gate_c_block = gate_a_block + D_MODEL // tn
    return pl.pallas_call(
        functools.partial(_merge_kernel, tn=tn),
        grid=(n // tn, s // tm),
        in_specs=[pl.BlockSpec((tm, k), lambda j, i: (i, 0)),
                  pl.BlockSpec((tm, k), lambda j, i: (i, 0)),
                  pl.BlockSpec(memory_space=pl.ANY),
                  pl.BlockSpec(memory_space=pl.ANY),
                  pl.BlockSpec((tm, tn), lambda j, i: (i, j + gate_a_block)),
                  pl.BlockSpec((tm, tn), lambda j, i: (i, j + gate_c_block))],
        out_specs=pl.BlockSpec((tm, tn), lambda j, i: (i, j)),
        out_shape=jax.ShapeDtypeStruct((s, n), BF16),
        scratch_shapes=_weight_stage_scratch(k, tn) + _weight_stage_scratch(k, tn),
        compiler_params=_params("arbitrary", "arbitrary"),
        name="gated_merge",
    )(a_att, a_conv, w_att_out, w_conv_out, rest, rest)


def _ple_kernel(x1_ref, wg_ref, p_ref, wp_ref, ggate_ref, gpost_ref, gfin_ref, o_ref, hg_ref, *, tm, tn):
    def gate_norm(rows):
        xr = x1_ref[rows, :]
        hg_ref[rows, :] = (xr * _rms_scale(xr, RMS_EPS) * ggate_ref[...]).astype(BF16)
    _for_row_chunks(tm, 16, gate_norm)

    o_ref[...] = jnp.dot(p_ref[...].astype(BF16), wp_ref[...], preferred_element_type=F32)

    def embed_norm(rows):
        er = o_ref[rows, :]
        o_ref[rows, :] = er * _rms_scale(er, RMS_EPS) * gpost_ref[...]
    _for_row_chunks(tm, 8, embed_norm)

    for c in range(D_MODEL // tn):
        cols = slice(c * tn, (c + 1) * tn)
        g = _sigmoid(jnp.dot(hg_ref[...], wg_ref[:, cols], preferred_element_type=F32))
        o_ref[:, cols] = x1_ref[:, cols] + g * o_ref[:, cols]

    def final_norm(rows):
        xr = o_ref[rows, :]
        o_ref[rows, :] = xr * _rms_scale(xr, RMS_EPS) * gfin_ref[...]
    _for_row_chunks(tm, 8, final_norm)


def _ple(x1, w_ple_gate, p, w_ple, g_ple_gate, g_ple_post, g_final, tm=256, tn=512):
    s, n = x1.shape
    vec = pl.BlockSpec((1, n), lambda i: (0, 0))
    resident = pl.Buffered(1)
    return pl.pallas_call(
        functools.partial(_ple_kernel, tm=tm, tn=tn),
        grid=(s // tm,),
        in_specs=[pl.BlockSpec((tm, n), lambda i: (i, 0)),
                  pl.BlockSpec((n, n), lambda i: (0, 0), pipeline_mode=resident),
                  pl.BlockSpec((tm, PLE_DIM), lambda i: (i, 0)),
                  pl.BlockSpec((PLE_DIM, n), lambda i: (0, 0), pipeline_mode=resident),
                  vec, vec, vec],
        out_specs=pl.BlockSpec((tm, n), lambda i: (i, 0)),
        out_shape=jax.ShapeDtypeStruct((s, n), F32),
        scratch_shapes=[pltpu.VMEM((tm, n), BF16)],
        compiler_params=_params("parallel"),
        name="ple_final",
    )(x1, w_ple_gate, p, w_ple, g_ple_gate.reshape(1, n), g_ple_post.reshape(1, n), g_final.reshape(1, n))


def kernel(x, p, g_mix, w_in, lambda_q1, lambda_k1, lambda_q2, lambda_k2, g_subln, w_att_out, conv_w,
           conv_b, ln_g, ln_b, w_conv_out, w_out, g_ple_gate, w_ple_gate, w_ple, g_ple_post, g_final):
    b, s, d = x.shape
    assert (b, s, d) == (1, SEQ, D_MODEL) and p.shape[0] == 1 and w_in.shape[0] == 1
    x2d = x.reshape(s, d)

    h, k = _norm_proj(x2d, g_mix[0], w_in[0], first_col=ATT_WIDTH, n_cols=ATT_WIDTH)
    tn = 1024
    att_blocks = ATT_WIDTH // tn
    q_blocks = tuple(range(att_blocks))
    v_blocks = tuple(range(2 * att_blocks, 3 * att_blocks))
    rest_blocks = tuple(range(3 * att_blocks, w_in.shape[-1] // tn))
    qvt = _matmul(h, w_in[0], q_blocks + v_blocks, BF16, tm=1024, tn=tn, name="proj_qv_t",
                  n_scaled_blocks=att_blocks, scale=QK_LOG2_SCALE, transpose_out=True)
    rest = _matmul(h, w_in[0], rest_blocks, F32, tm=1024, tn=tn, name="proj_rest")

    a_att = _attention(qvt, k, rest, lambda_q1[0], lambda_k1[0], lambda_q2[0], lambda_k2[0], g_subln[0])
    a_conv = _conv_branch(rest, conv_w[0], conv_b[0], ln_g[0], ln_b[0])
    m = _merge(a_att, a_conv, w_att_out[0], w_conv_out[0], rest)
    x1 = _matmul(m, w_out[0], tuple(range(D_MODEL // tn)), F32, tm=512, tn=tn, name="out_proj", residual=x2d)
    out = _ple(x1, w_ple_gate[0].astype(BF16), p[0, 0], w_ple[0].astype(BF16),
               g_ple_gate[0], g_ple_post[0], g_final)
    return out.reshape(b, s, d)
```

```python
import functools
import math

import jax
import jax.numpy as jnp
from jax import lax
from jax.experimental import pallas as pl
from jax.experimental.pallas import tpu as pltpu

D_MODEL = 4096
SEQ = 8192
PLE_DIM = 256
N_HEADS = 8
HEAD_DIM = 128
V_DIM = 2 * HEAD_DIM
ATT_WIDTH = N_HEADS * V_DIM
CONV_WIDTH = D_MODEL // 2
CONV_KERNEL = 31

RMS_EPS = 1e-6
SUBLN_EPS = 1e-5
LN_EPS = 1e-5
LAMBDA_INIT = 0.8 - 0.6 * math.exp(-0.3 * 0)
QK_LOG2_SCALE = HEAD_DIM ** -0.5 * math.log2(math.e)

VMEM_LIMIT_BYTES = 56 * 1024 * 1024
CONV_HALO = 32

BF16 = jnp.bfloat16
F32 = jnp.float32


def _params(*semantics):
    return pltpu.CompilerParams(dimension_semantics=semantics, vmem_limit_bytes=VMEM_LIMIT_BYTES)


def _rms_scale(v, eps):
    return lax.rsqrt(jnp.mean(v * v, axis=-1, keepdims=True) + eps)


def _sigmoid(v):
    return 0.5 * jnp.tanh(0.5 * v) + 0.5


def _silu(v):
    half = 0.5 * v
    return half * jnp.tanh(half) + half


def _for_row_chunks(n_rows, chunk, body):
    for r in range(0, n_rows, chunk):
        body(slice(r, r + chunk))


def _norm_proj_kernel(x_ref, g_ref, w_hbm_ref, h_ref, o_ref, wf_ref, wb_ref, sem, *, tm, first_col, stage_cols):
    @pl.when(pl.program_id(0) == 0)
    def _():
        for c in range(0, wb_ref.shape[1], stage_cols):
            piece = pltpu.make_async_copy(w_hbm_ref.at[:, pl.ds(first_col + c, stage_cols)], wf_ref, sem)
            piece.start()
            piece.wait()
            wb_ref[:, c:c + stage_cols] = wf_ref[...].astype(BF16)

    def norm(rows):
        xr = x_ref[rows, :]
        h_ref[rows, :] = (xr * _rms_scale(xr, RMS_EPS) * g_ref[...]).astype(BF16)
    _for_row_chunks(tm, 16, norm)
    o_ref[...] = jnp.dot(h_ref[...], wb_ref[...], preferred_element_type=F32).astype(o_ref.dtype)


def _norm_proj(x, g, w, first_col, n_cols, tm=512, stage_cols=512):
    s, d = x.shape
    return pl.pallas_call(
        functools.partial(_norm_proj_kernel, tm=tm, first_col=first_col, stage_cols=stage_cols),
        grid=(s // tm,),
        in_specs=[pl.BlockSpec((tm, d), lambda i: (i, 0)),
                  pl.BlockSpec((1, d), lambda i: (0, 0)),
                  pl.BlockSpec(memory_space=pl.ANY)],
        out_specs=[pl.BlockSpec((tm, d), lambda i: (i, 0)),
                   pl.BlockSpec((tm, n_cols), lambda i: (i, 0))],
        out_shape=[jax.ShapeDtypeStruct((s, d), BF16), jax.ShapeDtypeStruct((s, n_cols), BF16)],
        scratch_shapes=[pltpu.VMEM((d, stage_cols), F32), pltpu.VMEM((d, n_cols), BF16),
                        pltpu.SemaphoreType.DMA(())],
        compiler_params=_params("arbitrary"),
        name="norm_proj_k",
    )(x, g.reshape(1, d), w)


def _weight_col_block(j, col_blocks):
    block = j + col_blocks[0]
    for pos in range(1, len(col_blocks)):
        gap = col_blocks[pos] - col_blocks[pos - 1] - 1
        if gap:
            block = block + jnp.where(j >= pos, gap, 0)
    return block


def _stage_weight_block(w_hbm_ref, wf_ref, wb_ref, sem, col_blocks, tn):
    j = pl.program_id(0)

    def weight_copy(step):
        col = pl.multiple_of(_weight_col_block(step, col_blocks) * tn, tn)
        return pltpu.make_async_copy(w_hbm_ref.at[:, pl.ds(col, tn)], wf_ref, sem)

    @pl.when(pl.program_id(1) == 0)
    def _():
        @pl.when(j == 0)
        def _():
            weight_copy(0).start()

        weight_copy(j).wait()
        wb_ref[...] = wf_ref[...].astype(BF16)

        @pl.when(j + 1 < pl.num_programs(0))
        def _():
            weight_copy(j + 1).start()


def _weight_stage_scratch(k, tn):
    return [pltpu.VMEM((k, tn), F32), pltpu.VMEM((k, tn), BF16), pltpu.SemaphoreType.DMA(())]


def _matmul_kernel(*refs, col_blocks, tn, n_scaled_blocks, scale, has_residual, transpose_out):
    if has_residual:
        a_ref, w_hbm_ref, x_ref, o_ref, wf_ref, wb_ref, sem = refs
    else:
        a_ref, w_hbm_ref, o_ref, wf_ref, wb_ref, sem = refs
    _stage_weight_block(w_hbm_ref, wf_ref, wb_ref, sem, col_blocks, tn)
    acc = jnp.dot(a_ref[...], wb_ref[...], preferred_element_type=F32)
    if n_scaled_blocks:
        acc = acc * jnp.where(pl.program_id(0) < n_scaled_blocks, scale, 1.0)
    if has_residual:
        acc = x_ref[...] + acc
    if transpose_out:
        acc = acc.T
    o_ref[...] = acc.astype(o_ref.dtype)


def _matmul(a, w, col_blocks, out_dtype, tm, tn, name, n_scaled_blocks=0, scale=1.0, residual=None,
            transpose_out=False):
    s, k = a.shape
    n_cols = len(col_blocks) * tn
    tile = pl.BlockSpec((tm, tn), lambda j, i: (i, j))
    in_specs = [pl.BlockSpec((tm, k), lambda j, i: (i, 0)), pl.BlockSpec(memory_space=pl.ANY)]
    operands = [a, w]
    if residual is not None:
        in_specs.append(tile)
        operands.append(residual)
    if transpose_out:
        out_spec, out_shape = pl.BlockSpec((tn, tm), lambda j, i: (j, i)), (n_cols, s)
    else:
        out_spec, out_shape = tile, (s, n_cols)
    return pl.pallas_call(
        functools.partial(_matmul_kernel, col_blocks=tuple(col_blocks), tn=tn, n_scaled_blocks=n_scaled_blocks,
                          scale=scale, has_residual=residual is not None, transpose_out=transpose_out),
        grid=(len(col_blocks), s // tm),
        in_specs=in_specs,
        out_specs=out_spec,
        out_shape=jax.ShapeDtypeStruct(out_shape, out_dtype),
        scratch_shapes=_weight_stage_scratch(k, tn),
        compiler_params=_params("arbitrary", "arbitrary"),
        name=name,
    )(*operands)


def _attn_kernel(lq1_ref, lk1_ref, lq2_ref, lk2_ref, gsub_ref, qt_ref, qt_next_ref, k_ref, vt_ref, za_ref,
                 o_ref, acc1_ref, acc2_ref, sa_ref, sb_ref, ma_ref, mb_ref, *, tq):
    tk = tq // 2
    qi = pl.program_id(1)
    lam = (jnp.exp(jnp.sum(lq1_ref[...] * lk1_ref[...], axis=-1, keepdims=True))
           - jnp.exp(jnp.sum(lq2_ref[...] * lk2_ref[...], axis=-1, keepdims=True))
           + LAMBDA_INIT)
    acc_refs = (acc1_ref, acc2_ref)
    acc1_ref[...] = jnp.zeros_like(acc1_ref)
    acc2_ref[...] = jnp.zeros_like(acc2_ref)

    def qk(c, bufs, q0=0, queries_ref=qt_ref, with_max=True):
        s_ref, smax_ref = bufs
        kc = k_ref[pl.ds(pl.multiple_of(c * tk, tk), tk), :]
        for comp in range(2):
            dims = slice(comp * HEAD_DIM, (comp + 1) * HEAD_DIM)
            s = jnp.dot(kc[:, dims], queries_ref[dims, q0:], preferred_element_type=F32)
            s_ref[comp, :, q0:] = s
            if with_max:
                smax_ref[comp, :, q0:] = jnp.max(s, axis=0, keepdims=True)

    def softmax_pv(c, bufs, carry, masked, q0=0):
        s_ref, smax_ref = bufs
        vtc = vt_ref[:, pl.ds(pl.multiple_of(c * tk, tk), tk)]
        new = []
        for comp in range(2):
            m, l = carry[comp]
            s = s_ref[comp, :, q0:]
            if masked:
                key = c * tk + lax.broadcasted_iota(jnp.int32, (tk, tk), 0)
                query = qi * tq + q0 + lax.broadcasted_iota(jnp.int32, (tk, tk), 1)
                causal = jnp.where(key <= query, s[:, :tk], -jnp.inf)
                s = causal if s.shape[1] == tk else jnp.concatenate([causal, s[:, tk:]], axis=1)
                chunk_max = jnp.max(s, axis=0, keepdims=True)
            else:
                chunk_max = smax_ref[comp, :, q0:]
            m_new = jnp.maximum(m[:, q0:], chunk_max)
            alpha = jnp.exp2(m[:, q0:] - m_new)
            p = jnp.exp2(s - m_new)
            l_new = alpha * l[:, q0:] + jnp.sum(p, axis=0, keepdims=True)
            acc_ref = acc_refs[comp]
            acc_ref[:, q0:] = alpha * acc_ref[:, q0:] + jnp.dot(vtc, p.astype(BF16), preferred_element_type=F32)
            if q0:
                m_new = jnp.concatenate([m[:, :q0], m_new], axis=1)
                l_new = jnp.concatenate([l[:, :q0], l_new], axis=1)
            new.append((m_new, l_new))
        return tuple(new)

    buf_a, buf_b = (sa_ref, ma_ref), (sb_ref, mb_ref)

    def pair(j, carry):
        c0 = 2 * j
        qk(c0 + 1, buf_b)
        carry = softmax_pv(c0, buf_a, carry, masked=False)
        qk(c0 + 2, buf_a)
        return softmax_pv(c0 + 1, buf_b, carry, masked=False)

    neg = jnp.full((1, tq), -jnp.inf, F32)
    zero = jnp.zeros((1, tq), F32)

    @pl.when(qi == 0)
    def _():
        qk(0, buf_a, with_max=False)

    carry = lax.fori_loop(0, qi, pair, ((neg, zero), (neg, zero)))
    qk(2 * qi + 1, buf_b, q0=tk, with_max=False)
    carry = softmax_pv(2 * qi, buf_a, carry, masked=True)
    (_, l1), (_, l2) = softmax_pv(2 * qi + 1, buf_b, carry, masked=True, q0=tk)

    qk(0, buf_a, queries_ref=qt_next_ref)

    ot = acc1_ref[...] * (1.0 / l1) - acc2_ref[...] * (lam / l2)
    ot = ot * lax.rsqrt(jnp.mean(ot * ot, axis=0, keepdims=True) + SUBLN_EPS)
    o = ot.T * gsub_ref[...] * (1.0 - LAMBDA_INIT)
    o_ref[...] = (o * _silu(za_ref[...])).astype(o_ref.dtype)


def _attention(qvt, k, rest, lq1, lk1, lq2, lk2, g_subln, tq=1024):
    s = k.shape[0]
    n_tiles = s // tq
    vec = pl.BlockSpec((1, HEAD_DIM), lambda h, i: (0, 0))
    score_buf = pltpu.VMEM((2, tq // 2, tq), F32)
    score_max = pltpu.VMEM((2, 1, tq), F32)
    return pl.pallas_call(
        functools.partial(_attn_kernel, tq=tq),
        grid=(N_HEADS, n_tiles),
        in_specs=[vec, vec, vec, vec,
                  pl.BlockSpec((1, V_DIM), lambda h, i: (0, 0)),
                  pl.BlockSpec((V_DIM, tq), lambda h, i: (h, i)),
                  pl.BlockSpec((V_DIM, tq), lambda h, i: (h, jnp.minimum(i + 1, n_tiles - 1))),
                  pl.BlockSpec((s, V_DIM), lambda h, i: (0, h)),
                  pl.BlockSpec((V_DIM, s), lambda h, i: (N_HEADS + h, 0)),
                  pl.BlockSpec((tq, V_DIM), lambda h, i: (i, h))],
        out_specs=pl.BlockSpec((tq, V_DIM), lambda h, i: (i, h)),
        out_shape=jax.ShapeDtypeStruct((s, ATT_WIDTH), BF16),
        scratch_shapes=[pltpu.VMEM((V_DIM, tq), F32), pltpu.VMEM((V_DIM, tq), F32), score_buf, score_buf,
                        score_max, score_max],
        compiler_params=_params("parallel", "arbitrary"),
        name="diff_attention",
    )(lq1.reshape(1, -1), lk1.reshape(1, -1), lq2.reshape(1, -1), lk2.reshape(1, -1),
      g_subln.reshape(1, -1), qvt, qvt, k, qvt, rest)


def _conv_kernel(ga_ref, gb_ref, ga_halo_ref, gb_halo_ref, zc_ref, w_ref, b_ref, lng_ref, lnb_ref,
                 o_ref, ext_ref, y_ref, *, tr, lane_chunk, row_chunk):
    i = pl.program_id(0)
    halo = ga_halo_ref[...] * _sigmoid(gb_halo_ref[...])
    ext_ref[0:CONV_HALO, :] = jnp.where(i > 0, halo, 0.0)

    def glu(rows):
        ext_ref[CONV_HALO + rows.start:CONV_HALO + rows.stop, :] = ga_ref[rows, :] * _sigmoid(gb_ref[rows, :])
    _for_row_chunks(tr, 16, glu)

    first_tap = CONV_HALO - (CONV_KERNEL - 1)
    slab_rows = row_chunk + CONV_HALO

    def lane_body(c, _):
        lanes = pl.ds(pl.multiple_of(c * lane_chunk, lane_chunk), lane_chunk)
        for r0 in range(0, tr, row_chunk):
            slab = ext_ref[r0:r0 + slab_rows, lanes]
            acc = jnp.broadcast_to(b_ref[:, lanes], (row_chunk, lane_chunk))
            for phase in range(8):
                shifted = slab if phase == 0 else pltpu.roll(slab, slab_rows - phase, 0)
                for j in range(CONV_KERNEL):
                    if (first_tap + j) % 8 == phase:
                        off = first_tap + j - phase
                        acc = acc + w_ref[j:j + 1, lanes] * shifted[off:off + row_chunk]
            y_ref[r0:r0 + row_chunk, lanes] = acc
        return 0

    lax.fori_loop(0, CONV_WIDTH // lane_chunk, lane_body, 0)

    def norm_gate(rows):
        y = y_ref[rows, :]
        yc = y - jnp.mean(y, axis=-1, keepdims=True)
        ln = yc * lax.rsqrt(jnp.mean(yc * yc, axis=-1, keepdims=True) + LN_EPS) * lng_ref[...] + lnb_ref[...]
        o_ref[rows, :] = (_silu(ln) * _silu(zc_ref[rows, :])).astype(o_ref.dtype)
    _for_row_chunks(tr, 16, norm_gate)


def _conv_branch(rest, conv_w, conv_b, ln_g, ln_b, tr=512):
    s = rest.shape[0]
    c = CONV_WIDTH
    halo_blocks = tr // CONV_HALO
    row = lambda col: pl.BlockSpec((tr, c), lambda i: (i, col))
    halo = lambda col: pl.BlockSpec((CONV_HALO, c), lambda i: (jnp.maximum(i * halo_blocks - 1, 0), col))
    vec = pl.BlockSpec((1, c), lambda i: (0, 0))
    return pl.pallas_call(
        functools.partial(_conv_kernel, tr=tr, lane_chunk=128, row_chunk=128),
        grid=(s // tr,),
        in_specs=[row(1), row(2), halo(1), halo(2), row(3),
                  pl.BlockSpec((CONV_KERNEL, c), lambda i: (0, 0)), vec, vec, vec],
        out_specs=pl.BlockSpec((tr, c), lambda i: (i, 0)),
        out_shape=jax.ShapeDtypeStruct((s, c), BF16),
        scratch_shapes=[pltpu.VMEM((CONV_HALO + tr, c), F32), pltpu.VMEM((tr, c), F32)],
        compiler_params=_params("parallel"),
        name="conv_branch",
    )(rest, rest, rest, rest, rest, conv_w, conv_b.reshape(1, c), ln_g.reshape(1, c), ln_b.reshape(1, c))


def _merge_kernel(aa_ref, ac_ref, wa_hbm_ref, wc_hbm_ref, ga_ref, gc_ref, o_ref,
                  waf_ref, wab_ref, sem_a, wcf_ref, wcb_ref, sem_c, *, tn):
    col_blocks = tuple(range(D_MODEL // tn))
    _stage_weight_block(wa_hbm_ref, waf_ref, wab_ref, sem_a, col_blocks, tn)
    _stage_weight_block(wc_hbm_ref, wcf_ref, wcb_ref, sem_c, col_blocks, tn)
    ya = jnp.dot(aa_ref[...], wab_ref[...], preferred_element_type=F32)
    yc = jnp.dot(ac_ref[...], wcb_ref[...], preferred_element_type=F32)
    m = _sigmoid(ga_ref[...]) * ya + _sigmoid(gc_ref[...]) * yc
    o_ref[...] = m.astype(o_ref.dtype)


def _merge(a_att, a_conv, w_att_out, w_conv_out, rest, tm=512, tn=1024):
    s, k = a_att.shape
    n = D_MODEL
    gate_a_block = (ATT_WIDTH + 3 * CONV_WIDTH) // tn
    gate_c_block = gate_a_block + D_MODEL // tn
    return pl.pallas_call(
        functools.partial(_merge_kernel, tn=tn),
        grid=(n // tn, s // tm),
        in_specs=[pl.BlockSpec((tm, k), lambda j, i: (i, 0)),
                  pl.BlockSpec((tm, k), lambda j, i: (i, 0)),
                  pl.BlockSpec(memory_space=pl.ANY),
                  pl.BlockSpec(memory_space=pl.ANY),
                  pl.BlockSpec((tm, tn), lambda j, i: (i, j + gate_a_block)),
                  pl.BlockSpec((tm, tn), lambda j, i: (i, j + gate_c_block))],
        out_specs=pl.BlockSpec((tm, tn), lambda j, i: (i, j)),
        out_shape=jax.ShapeDtypeStruct((s, n), BF16),
        scratch_shapes=_weight_stage_scratch(k, tn) + _weight_stage_scratch(k, tn),
        compiler_params=_params("arbitrary", "arbitrary"),
        name="gated_merge",
    )(a_att, a_conv, w_att_out, w_conv_out, rest, rest)


def _ple_kernel(x1_ref, wg_ref, p_ref, wp_ref, ggate_ref, gpost_ref, gfin_ref, o_ref, hg_ref, *, tm, tn):
    def gate_norm(rows):
        xr = x1_ref[rows, :]
        hg_ref[rows, :] = (xr * _rms_scale(xr, RMS_EPS) * ggate_ref[...]).astype(BF16)
    _for_row_chunks(tm, 16, gate_norm)

    o_ref[...] = jnp.dot(p_ref[...].astype(BF16), wp_ref[...], preferred_element_type=F32)

    def embed_norm(rows):
        er = o_ref[rows, :]
        o_ref[rows, :] = er * _rms_scale(er, RMS_EPS) * gpost_ref[...]
    _for_row_chunks(tm, 8, embed_norm)

    for c in range(D_MODEL // tn):
        cols = slice(c * tn, (c + 1) * tn)
        g = _sigmoid(jnp.dot(hg_ref[...], wg_ref[:, cols], preferred_element_type=F32))
        o_ref[:, cols] = x1_ref[:, cols] + g * o_ref[:, cols]

    def final_norm(rows):
        xr = o_ref[rows, :]
        o_ref[rows, :] = xr * _rms_scale(xr, RMS_EPS) * gfin_ref[...]
    _for_row_chunks(tm, 8, final_norm)


def _ple(x1, w_ple_gate, p, w_ple, g_ple_gate, g_ple_post, g_final, tm=256, tn=512):
    s, n = x1.shape
    vec = pl.BlockSpec((1, n), lambda i: (0, 0))
    resident = pl.Buffered(1)
    return pl.pallas_call(
        functools.partial(_ple_kernel, tm=tm, tn=tn),
        grid=(s // tm,),
        in_specs=[pl.BlockSpec((tm, n), lambda i: (i, 0)),
                  pl.BlockSpec((n, n), lambda i: (0, 0), pipeline_mode=resident),
                  pl.BlockSpec((tm, PLE_DIM), lambda i: (i, 0)),
                  pl.BlockSpec((PLE_DIM, n), lambda i: (0, 0), pipeline_mode=resident),
                  vec, vec, vec],
        out_specs=pl.BlockSpec((tm, n), lambda i: (i, 0)),
        out_shape=jax.ShapeDtypeStruct((s, n), F32),
        scratch_shapes=[pltpu.VMEM((tm, n), BF16)],
        compiler_params=_params("parallel"),
        name="ple_final",
    )(x1, w_ple_gate, p, w_ple, g_ple_gate.reshape(1, n), g_ple_post.reshape(1, n), g_final.reshape(1, n))


def kernel(x, p, g_mix, w_in, lambda_q1, lambda_k1, lambda_q2, lambda_k2, g_subln, w_att_out, conv_w,
           conv_b, ln_g, ln_b, w_conv_out, w_out, g_ple_gate, w_ple_gate, w_ple, g_ple_post, g_final):
    b, s, d = x.shape
    assert (b, s, d) == (1, SEQ, D_MODEL) and p.shape[0] == 1 and w_in.shape[0] == 1
    x2d = x.reshape(s, d)

    h, k = _norm_proj(x2d, g_mix[0], w_in[0], first_col=ATT_WIDTH, n_cols=ATT_WIDTH)
    tn = 1024
    att_blocks = ATT_WIDTH // tn
    q_blocks = tuple(range(att_blocks))
    v_blocks = tuple(range(2 * att_blocks, 3 * att_blocks))
    rest_blocks = tuple(range(3 * att_blocks, w_in.shape[-1] // tn))
    qvt = _matmul(h, w_in[0], q_blocks + v_blocks, BF16, tm=1024, tn=tn, name="proj_qv_t",
                  n_scaled_blocks=att_blocks, scale=QK_LOG2_SCALE, transpose_out=True)
    rest = _matmul(h, w_in[0], rest_blocks, F32, tm=1024, tn=tn, name="proj_rest")

    a_att = _attention(qvt, k, rest, lambda_q1[0], lambda_k1[0], lambda_q2[0], lambda_k2[0], g_subln[0])
    a_conv = _conv_branch(rest, conv_w[0], conv_b[0], ln_g[0], ln_b[0])
    m = _merge(a_att, a_conv, w_att_out[0], w_conv_out[0], rest)
    x1 = _matmul(m, w_out[0], tuple(range(D_MODEL // tn)), F32, tm=512, tn=tn, name="out_proj", residual=x2d)
    out = _ple(x1, w_ple_gate[0].astype(BF16), p[0, 0], w_ple[0].astype(BF16),
               g_ple_gate[0], g_ple_post[0], g_final)
    return out.reshape(b, s, d)
```

```python
import functools
import math

import jax
import jax.numpy as jnp
from jax import lax
from jax.experimental import pallas as pl
from jax.experimental.pallas import tpu as pltpu

D_MODEL = 4096
SEQ = 8192
PLE_DIM = 256
N_HEADS = 8
HEAD_DIM = 128
V_DIM = 2 * HEAD_DIM
ATT_WIDTH = N_HEADS * V_DIM
CONV_WIDTH = D_MODEL // 2
CONV_KERNEL = 31

RMS_EPS = 1e-6
SUBLN_EPS = 1e-5
LN_EPS = 1e-5
LAMBDA_INIT = 0.8 - 0.6 * math.exp(-0.3 * 0)
QK_LOG2_SCALE = HEAD_DIM ** -0.5 * math.log2(math.e)

VMEM_LIMIT_BYTES = 56 * 1024 * 1024
CONV_HALO = 32

BF16 = jnp.bfloat16
F32 = jnp.float32


def _params(*semantics):
    return pltpu.CompilerParams(dimension_semantics=semantics, vmem_limit_bytes=VMEM_LIMIT_BYTES)


def _rms_scale(v, eps):
    return lax.rsqrt(jnp.mean(v * v, axis=-1, keepdims=True) + eps)


def _sigmoid(v):
    return 0.5 * jnp.tanh(0.5 * v) + 0.5


def _silu(v):
    half = 0.5 * v
    return half * jnp.tanh(half) + half


def _for_row_chunks(n_rows, chunk, body):
    for r in range(0, n_rows, chunk):
        body(slice(r, r + chunk))


def _norm_proj_kernel(x_ref, g_ref, w_hbm_ref, h_ref, o_ref, wf_ref, wb_ref, sem, *, tm, first_col, stage_cols):
    @pl.when(pl.program_id(0) == 0)
    def _():
        for c in range(0, wb_ref.shape[1], stage_cols):
            piece = pltpu.make_async_copy(w_hbm_ref.at[:, pl.ds(first_col + c, stage_cols)], wf_ref, sem)
            piece.start()
            piece.wait()
            wb_ref[:, c:c + stage_cols] = wf_ref[...].astype(BF16)

    def norm(rows):
        xr = x_ref[rows, :]
        h_ref[rows, :] = (xr * _rms_scale(xr, RMS_EPS) * g_ref[...]).astype(BF16)
    _for_row_chunks(tm, 16, norm)
    o_ref[...] = jnp.dot(h_ref[...], wb_ref[...], preferred_element_type=F32).astype(o_ref.dtype)


def _norm_proj(x, g, w, first_col, n_cols, tm=512, stage_cols=512):
    s, d = x.shape
    return pl.pallas_call(
        functools.partial(_norm_proj_kernel, tm=tm, first_col=first_col, stage_cols=stage_cols),
        grid=(s // tm,),
        in_specs=[pl.BlockSpec((tm, d), lambda i: (i, 0)),
                  pl.BlockSpec((1, d), lambda i: (0, 0)),
                  pl.BlockSpec(memory_space=pl.ANY)],
        out_specs=[pl.BlockSpec((tm, d), lambda i: (i, 0)),
                   pl.BlockSpec((tm, n_cols), lambda i: (i, 0))],
        out_shape=[jax.ShapeDtypeStruct((s, d), BF16), jax.ShapeDtypeStruct((s, n_cols), BF16)],
        scratch_shapes=[pltpu.VMEM((d, stage_cols), F32), pltpu.VMEM((d, n_cols), BF16),
                        pltpu.SemaphoreType.DMA(())],
        compiler_params=_params("arbitrary"),
        name="norm_proj_k",
    )(x, g.reshape(1, d), w)


def _weight_col_block(j, col_blocks):
    block = j + col_blocks[0]
    for pos in range(1, len(col_blocks)):
        gap = col_blocks[pos] - col_blocks[pos - 1] - 1
        if gap:
            block = block + jnp.where(j >= pos, gap, 0)
    return block


def _stage_weight_block(w_hbm_ref, wf_ref, wb_ref, sem, col_blocks, tn):
    j = pl.program_id(0)

    def weight_copy(step):
        col = pl.multiple_of(_weight_col_block(step, col_blocks) * tn, tn)
        return pltpu.make_async_copy(w_hbm_ref.at[:, pl.ds(col, tn)], wf_ref, sem)

    @pl.when(pl.program_id(1) == 0)
    def _():
        @pl.when(j == 0)
        def _():
            weight_copy(0).start()

        weight_copy(j).wait()
        wb_ref[...] = wf_ref[...].astype(BF16)

        @pl.when(j + 1 < pl.num_programs(0))
        def _():
            weight_copy(j + 1).start()


def _weight_stage_scratch(k, tn):
    return [pltpu.VMEM((k, tn), F32), pltpu.VMEM((k, tn), BF16), pltpu.SemaphoreType.DMA(())]


def _matmul_kernel(*refs, col_blocks, tn, n_scaled_blocks, scale, has_residual, has_side_cast, transpose_out):
    refs = list(refs)
    a_ref, w_hbm_ref = refs[:2]
    wf_ref, wb_ref, sem = refs[-3:]
    mid = refs[2:-3]
    x_ref = mid.pop(0) if has_residual else None
    if has_side_cast:
        side_in_ref, o_ref, side_out_ref = mid
        side_out_ref[...] = side_in_ref[...].astype(side_out_ref.dtype)
    else:
        o_ref, = mid
    _stage_weight_block(w_hbm_ref, wf_ref, wb_ref, sem, col_blocks, tn)
    acc = jnp.dot(a_ref[...], wb_ref[...], preferred_element_type=F32)
    if n_scaled_blocks:
        acc = acc * jnp.where(pl.program_id(0) < n_scaled_blocks, scale, 1.0)
    if has_residual:
        acc = x_ref[...] + acc
    if transpose_out:
        acc = acc.T
    o_ref[...] = acc.astype(o_ref.dtype)


def _matmul(a, w, col_blocks, out_dtype, tm, tn, name, n_scaled_blocks=0, scale=1.0, residual=None,
            transpose_out=False, side_cast=None):
    s, k = a.shape
    n_cols = len(col_blocks) * tn
    grid = (len(col_blocks), s // tm)
    tile = pl.BlockSpec((tm, tn), lambda j, i: (i, j))
    in_specs = [pl.BlockSpec((tm, k), lambda j, i: (i, 0)), pl.BlockSpec(memory_space=pl.ANY)]
    operands = [a, w]
    if residual is not None:
        in_specs.append(tile)
        operands.append(residual)
    if transpose_out:
        out_specs, out_shape = pl.BlockSpec((tn, tm), lambda j, i: (j, i)), (n_cols, s)
    else:
        out_specs, out_shape = tile, (s, n_cols)
    out_shape = jax.ShapeDtypeStruct(out_shape, out_dtype)
    if side_cast is not None:
        rows, cols = side_cast.shape
        slice_rows = rows // (grid[0] * grid[1])
        side_spec = pl.BlockSpec((slice_rows, cols), lambda j, i: (j * grid[1] + i, 0))
        in_specs.append(side_spec)
        operands.append(side_cast)
        out_specs, out_shape = [out_specs, side_spec], [out_shape, jax.ShapeDtypeStruct((rows, cols), BF16)]
    return pl.pallas_call(
        functools.partial(_matmul_kernel, col_blocks=tuple(col_blocks), tn=tn, n_scaled_blocks=n_scaled_blocks,
                          scale=scale, has_residual=residual is not None, has_side_cast=side_cast is not None,
                          transpose_out=transpose_out),
        grid=grid,
        in_specs=in_specs,
        out_specs=out_specs,
        out_shape=out_shape,
        scratch_shapes=_weight_stage_scratch(k, tn),
        compiler_params=_params("arbitrary", "arbitrary"),
        name=name,
    )(*operands)


def _attn_kernel(lq1_ref, lk1_ref, lq2_ref, lk2_ref, gsub_ref, qt_ref, qt_next_ref, k_ref, vt_ref, za_ref,
                 o_ref, acc1_ref, acc2_ref, sa_ref, sb_ref, ma_ref, mb_ref, *, tq):
    tk = tq // 2
    qi = pl.program_id(1)
    lam = (jnp.exp(jnp.sum(lq1_ref[...] * lk1_ref[...], axis=-1, keepdims=True))
           - jnp.exp(jnp.sum(lq2_ref[...] * lk2_ref[...], axis=-1, keepdims=True))
           + LAMBDA_INIT)
    acc_refs = (acc1_ref, acc2_ref)
    acc1_ref[...] = jnp.zeros_like(acc1_ref)
    acc2_ref[...] = jnp.zeros_like(acc2_ref)

    def qk(c, bufs, q0=0, queries_ref=qt_ref, with_max=True):
        s_ref, smax_ref = bufs
        kc = k_ref[pl.ds(pl.multiple_of(c * tk, tk), tk), :]
        for comp in range(2):
            dims = slice(comp * HEAD_DIM, (comp + 1) * HEAD_DIM)
            s = jnp.dot(kc[:, dims], queries_ref[dims, q0:], preferred_element_type=F32)
            s_ref[comp, :, q0:] = s
            if with_max:
                smax_ref[comp, :, q0:] = jnp.max(s, axis=0, keepdims=True)

    def softmax_pv(c, bufs, carry, masked, q0=0):
        s_ref, smax_ref = bufs
        vtc = vt_ref[:, pl.ds(pl.multiple_of(c * tk, tk), tk)]
        new = []
        for comp in range(2):
            m, l = carry[comp]
            s = s_ref[comp, :, q0:]
            if masked:
                key = c * tk + lax.broadcasted_iota(jnp.int32, (tk, tk), 0)
                query = qi * tq + q0 + lax.broadcasted_iota(jnp.int32, (tk, tk), 1)
                causal = jnp.where(key <= query, s[:, :tk], -jnp.inf)
                s = causal if s.shape[1] == tk else jnp.concatenate([causal, s[:, tk:]], axis=1)
                chunk_max = jnp.max(s, axis=0, keepdims=True)
            else:
                chunk_max = smax_ref[comp, :, q0:]
            m_new = jnp.maximum(m[:, q0:], chunk_max)
            alpha = jnp.exp2(m[:, q0:] - m_new)
            p = jnp.exp2(s - m_new)
            l_new = alpha * l[:, q0:] + jnp.sum(p, axis=0, keepdims=True)
            acc_ref = acc_refs[comp]
            acc_ref[:, q0:] = alpha * acc_ref[:, q0:] + jnp.dot(vtc, p.astype(BF16), preferred_element_type=F32)
            if q0:
                m_new = jnp.concatenate([m[:, :q0], m_new], axis=1)
                l_new = jnp.concatenate([l[:, :q0], l_new], axis=1)
            new.append((m_new, l_new))
        return tuple(new)

    buf_a, buf_b = (sa_ref, ma_ref), (sb_ref, mb_ref)

    def pair(j, carry):
        c0 = 2 * j
        qk(c0 + 1, buf_b)
        carry = softmax_pv(c0, buf_a, carry, masked=False)
        qk(c0 + 2, buf_a)
        return softmax_pv(c0 + 1, buf_b, carry, masked=False)

    neg = jnp.full((1, tq), -jnp.inf, F32)
    zero = jnp.zeros((1, tq), F32)

    @pl.when(qi == 0)
    def _():
        qk(0, buf_a, with_max=False)

    carry = lax.fori_loop(0, qi, pair, ((neg, zero), (neg, zero)))
    qk(2 * qi + 1, buf_b, q0=tk, with_max=False)
    carry = softmax_pv(2 * qi, buf_a, carry, masked=True)
    (_, l1), (_, l2) = softmax_pv(2 * qi + 1, buf_b, carry, masked=True, q0=tk)

    qk(0, buf_a, queries_ref=qt_next_ref)

    ot = acc1_ref[...] * (1.0 / l1) - acc2_ref[...] * (lam / l2)
    ot = ot * lax.rsqrt(jnp.mean(ot * ot, axis=0, keepdims=True) + SUBLN_EPS)
    o = ot.T * gsub_ref[...] * (1.0 - LAMBDA_INIT)
    o_ref[...] = (o * _silu(za_ref[...])).astype(o_ref.dtype)


def _attention(qvt, k, rest, lq1, lk1, lq2, lk2, g_subln, tq=1024):
    s = k.shape[0]
    n_tiles = s // tq
    vec = pl.BlockSpec((1, HEAD_DIM), lambda h, i: (0, 0))
    score_buf = pltpu.VMEM((2, tq // 2, tq), F32)
    score_max = pltpu.VMEM((2, 1, tq), F32)
    return pl.pallas_call(
        functools.partial(_attn_kernel, tq=tq),
        grid=(N_HEADS, n_tiles),
        in_specs=[vec, vec, vec, vec,
                  pl.BlockSpec((1, V_DIM), lambda h, i: (0, 0)),
                  pl.BlockSpec((V_DIM, tq), lambda h, i: (h, i)),
                  pl.BlockSpec((V_DIM, tq), lambda h, i: (h, jnp.minimum(i + 1, n_tiles - 1))),
                  pl.BlockSpec((s, V_DIM), lambda h, i: (0, h)),
                  pl.BlockSpec((V_DIM, s), lambda h, i: (N_HEADS + h, 0)),
                  pl.BlockSpec((tq, V_DIM), lambda h, i: (i, h))],
        out_specs=pl.BlockSpec((tq, V_DIM), lambda h, i: (i, h)),
        out_shape=jax.ShapeDtypeStruct((s, ATT_WIDTH), BF16),
        scratch_shapes=[pltpu.VMEM((V_DIM, tq), F32), pltpu.VMEM((V_DIM, tq), F32), score_buf, score_buf,
                        score_max, score_max],
        compiler_params=_params("parallel", "arbitrary"),
        name="diff_attention",
    )(lq1.reshape(1, -1), lk1.reshape(1, -1), lq2.reshape(1, -1), lk2.reshape(1, -1),
      g_subln.reshape(1, -1), qvt, qvt, k, qvt, rest)


def _conv_kernel(ga_ref, gb_ref, ga_halo_ref, gb_halo_ref, zc_ref, w_ref, b_ref, lng_ref, lnb_ref,
                 o_ref, ext_ref, y_ref, *, tr, lane_chunk, row_chunk):
    i = pl.program_id(0)
    halo = ga_halo_ref[...] * _sigmoid(gb_halo_ref[...])
    ext_ref[0:CONV_HALO, :] = jnp.where(i > 0, halo, 0.0)

    def glu(rows):
        ext_ref[CONV_HALO + rows.start:CONV_HALO + rows.stop, :] = ga_ref[rows, :] * _sigmoid(gb_ref[rows, :])
    _for_row_chunks(tr, 16, glu)

    first_tap = CONV_HALO - (CONV_KERNEL - 1)
    slab_rows = row_chunk + CONV_HALO

    def lane_body(c, _):
        lanes = pl.ds(pl.multiple_of(c * lane_chunk, lane_chunk), lane_chunk)
        for r0 in range(0, tr, row_chunk):
            slab = ext_ref[r0:r0 + slab_rows, lanes]
            acc = jnp.broadcast_to(b_ref[:, lanes], (row_chunk, lane_chunk))
            for phase in range(8):
                shifted = slab if phase == 0 else pltpu.roll(slab, slab_rows - phase, 0)
                for j in range(CONV_KERNEL):
                    if (first_tap + j) % 8 == phase:
                        off = first_tap + j - phase
                        acc = acc + w_ref[j:j + 1, lanes] * shifted[off:off + row_chunk]
            y_ref[r0:r0 + row_chunk, lanes] = acc
        return 0

    lax.fori_loop(0, CONV_WIDTH // lane_chunk, lane_body, 0)

    def norm_gate(rows):
        y = y_ref[rows, :]
        yc = y - jnp.mean(y, axis=-1, keepdims=True)
        ln = yc * lax.rsqrt(jnp.mean(yc * yc, axis=-1, keepdims=True) + LN_EPS) * lng_ref[...] + lnb_ref[...]
        o_ref[rows, :] = (_silu(ln) * _silu(zc_ref[rows, :])).astype(o_ref.dtype)
    _for_row_chunks(tr, 16, norm_gate)


def _conv_branch(rest, conv_w, conv_b, ln_g, ln_b, tr=512):
    s = rest.shape[0]
    c = CONV_WIDTH
    halo_blocks = tr // CONV_HALO
    row = lambda col: pl.BlockSpec((tr, c), lambda i: (i, col))
    halo = lambda col: pl.BlockSpec((CONV_HALO, c), lambda i: (jnp.maximum(i * halo_blocks - 1, 0), col))
    vec = pl.BlockSpec((1, c), lambda i: (0, 0))
    return pl.pallas_call(
        functools.partial(_conv_kernel, tr=tr, lane_chunk=128, row_chunk=128),
        grid=(s // tr,),
        in_specs=[row(1), row(2), halo(1), halo(2), row(3),
                  pl.BlockSpec((CONV_KERNEL, c), lambda i: (0, 0)), vec, vec, vec],
        out_specs=pl.BlockSpec((tr, c), lambda i: (i, 0)),
        out_shape=jax.ShapeDtypeStruct((s, c), BF16),
        scratch_shapes=[pltpu.VMEM((CONV_HALO + tr, c), F32), pltpu.VMEM((tr, c), F32)],
        compiler_params=_params("parallel"),
        name="conv_branch",
    )(rest, rest, rest, rest, rest, conv_w, conv_b.reshape(1, c), ln_g.reshape(1, c), ln_b.reshape(1, c))


def _merge_kernel(aa_ref, ac_ref, wa_hbm_ref, wc_hbm_ref, ga_ref, gc_ref, o_ref,
                  waf_ref, wab_ref, sem_a, wcf_ref, wcb_ref, sem_c, *, tn):
    col_blocks = tuple(range(D_MODEL // tn))
    _stage_weight_block(wa_hbm_ref, waf_ref, wab_ref, sem_a, col_blocks, tn)
    _stage_weight_block(wc_hbm_ref, wcf_ref, wcb_ref, sem_c, col_blocks, tn)
    ya = jnp.dot(aa_ref[...], wab_ref[...], preferred_element_type=F32)
    yc = jnp.dot(ac_ref[...], wcb_ref[...], preferred_element_type=F32)
    m = _sigmoid(ga_ref[...]) * ya + _sigmoid(gc_ref[...]) * yc
    o_ref[...] = m.astype(o_ref.dtype)


def _merge(a_att, a_conv, w_att_out, w_conv_out, rest, tm=512, tn=1024):
    s, k = a_att.shape
    n = D_MODEL
    gate_a_block = (ATT_WIDTH + 3 * CONV_WIDTH) // tn
    gate_c_block = gate_a_block + D_MODEL // tn
    return pl.pallas_call(
        functools.partial(_merge_kernel, tn=tn),
        grid=(n // tn, s // tm),
        in_specs=[pl.BlockSpec((tm, k), lambda j, i: (i, 0)),
                  pl.BlockSpec((tm, k), lambda j, i: (i, 0)),
                  pl.BlockSpec(memory_space=pl.ANY),
                  pl.BlockSpec(memory_space=pl.ANY),
                  pl.BlockSpec((tm, tn), lambda j, i: (i, j + gate_a_block)),
                  pl.BlockSpec((tm, tn), lambda j, i: (i, j + gate_c_block))],
        out_specs=pl.BlockSpec((tm, tn), lambda j, i: (i, j)),
        out_shape=jax.ShapeDtypeStruct((s, n), BF16),
        scratch_shapes=_weight_stage_scratch(k, tn) + _weight_stage_scratch(k, tn),
        compiler_params=_params("arbitrary", "arbitrary"),
        name="gated_merge",
    )(a_att, a_conv, w_att_out, w_conv_out, rest, rest)


def _ple_kernel(x1_ref, wg_ref, p_ref, wp_ref, ggate_ref, gpost_ref, gfin_ref, o_ref, hg_ref, *, tm, tn):
    def gate_norm(rows):
        xr = x1_ref[rows, :]
        hg_ref[rows, :] = (xr * _rms_scale(xr, RMS_EPS) * ggate_ref[...]).astype(BF16)
    _for_row_chunks(tm, 16, gate_norm)

    o_ref[...] = jnp.dot(p_ref[...].astype(BF16), wp_ref[...], preferred_element_type=F32)

    def embed_norm(rows):
        er = o_ref[rows, :]
        o_ref[rows, :] = er * _rms_scale(er, RMS_EPS) * gpost_ref[...]
    _for_row_chunks(tm, 8, embed_norm)

    for c in range(D_MODEL // tn):
        cols = slice(c * tn, (c + 1) * tn)
        g = _sigmoid(jnp.dot(hg_ref[...], wg_ref[:, cols], preferred_element_type=F32))
        o_ref[:, cols] = x1_ref[:, cols] + g * o_ref[:, cols]

    def final_norm(rows):
        xr = o_ref[rows, :]
        o_ref[rows, :] = xr * _rms_scale(xr, RMS_EPS) * gfin_ref[...]
    _for_row_chunks(tm, 8, final_norm)


def _ple(x1, w_ple_gate, p, w_ple, g_ple_gate, g_ple_post, g_final, tm=256, tn=512):
    s, n = x1.shape
    vec = pl.BlockSpec((1, n), lambda i: (0, 0))
    resident = pl.Buffered(1)
    return pl.pallas_call(
        functools.partial(_ple_kernel, tm=tm, tn=tn),
        grid=(s // tm,),
        in_specs=[pl.BlockSpec((tm, n), lambda i: (i, 0)),
                  pl.BlockSpec((n, n), lambda i: (0, 0), pipeline_mode=resident),
                  pl.BlockSpec((tm, PLE_DIM), lambda i: (i, 0)),
                  pl.BlockSpec((PLE_DIM, n), lambda i: (0, 0), pipeline_mode=resident),
                  vec, vec, vec],
        out_specs=pl.BlockSpec((tm, n), lambda i: (i, 0)),
        out_shape=jax.ShapeDtypeStruct((s, n), F32),
        scratch_shapes=[pltpu.VMEM((tm, n), BF16)],
        compiler_params=_params("parallel"),
        name="ple_final",
    )(x1, w_ple_gate, p, w_ple, g_ple_gate.reshape(1, n), g_ple_post.reshape(1, n), g_final.reshape(1, n))


def kernel(x, p, g_mix, w_in, lambda_q1, lambda_k1, lambda_q2, lambda_k2, g_subln, w_att_out, conv_w,
           conv_b, ln_g, ln_b, w_conv_out, w_out, g_ple_gate, w_ple_gate, w_ple, g_ple_post, g_final):
    b, s, d = x.shape
    assert (b, s, d) == (1, SEQ, D_MODEL) and p.shape[0] == 1 and w_in.shape[0] == 1
    x2d = x.reshape(s, d)

    h, k = _norm_proj(x2d, g_mix[0], w_in[0], first_col=ATT_WIDTH, n_cols=ATT_WIDTH)
    tn = 1024
    att_blocks = ATT_WIDTH // tn
    q_blocks = tuple(range(att_blocks))
    v_blocks = tuple(range(2 * att_blocks, 3 * att_blocks))
    rest_blocks = tuple(range(3 * att_blocks, w_in.shape[-1] // tn))
    qvt = _matmul(h, w_in[0], q_blocks + v_blocks, BF16, tm=1024, tn=tn, name="proj_qv_t",
                  n_scaled_blocks=att_blocks, scale=QK_LOG2_SCALE, transpose_out=True)
    rest = _matmul(h, w_in[0], rest_blocks, F32, tm=1024, tn=tn, name="proj_rest")

    a_att = _attention(qvt, k, rest, lambda_q1[0], lambda_k1[0], lambda_q2[0], lambda_k2[0], g_subln[0])
    a_conv = _conv_branch(rest, conv_w[0], conv_b[0], ln_g[0], ln_b[0])
    m = _merge(a_att, a_conv, w_att_out[0], w_conv_out[0], rest)
    x1, w_ple_gate_bf16 = _matmul(m, w_out[0], tuple(range(D_MODEL // tn)), F32, tm=512, tn=tn, name="out_proj",
                                  residual=x2d, side_cast=w_ple_gate[0])
    out = _ple(x1, w_ple_gate_bf16, p[0, 0], w_ple[0].astype(BF16),
               g_ple_gate[0], g_ple_post[0], g_final)
    return out.reshape(b, s, d)
```

```python
import functools
import math

import jax
import jax.numpy as jnp
from jax import lax
from jax.experimental import pallas as pl
from jax.experimental.pallas import tpu as pltpu

D_MODEL = 4096
SEQ = 8192
PLE_DIM = 256
N_HEADS = 8
HEAD_DIM = 128
V_DIM = 2 * HEAD_DIM
ATT_WIDTH = N_HEADS * V_DIM
CONV_WIDTH = D_MODEL // 2
CONV_KERNEL = 31

RMS_EPS = 1e-6
SUBLN_EPS = 1e-5
LN_EPS = 1e-5
LAMBDA_INIT = 0.8 - 0.6 * math.exp(-0.3 * 0)
QK_LOG2_SCALE = HEAD_DIM ** -0.5 * math.log2(math.e)

VMEM_LIMIT_BYTES = 56 * 1024 * 1024
CONV_HALO = 32

BF16 = jnp.bfloat16
F32 = jnp.float32


def _params(*semantics):
    return pltpu.CompilerParams(dimension_semantics=semantics, vmem_limit_bytes=VMEM_LIMIT_BYTES)


def _rms_scale(v, eps):
    return lax.rsqrt(jnp.mean(v * v, axis=-1, keepdims=True) + eps)


def _sigmoid(v):
    return 0.5 * jnp.tanh(0.5 * v) + 0.5


def _silu(v):
    half = 0.5 * v
    return half * jnp.tanh(half) + half


def _for_row_chunks(n_rows, chunk, body):
    for r in range(0, n_rows, chunk):
        body(slice(r, r + chunk))


def _norm_proj_kernel(x_ref, g_ref, w_hbm_ref, h_ref, o_ref, wf_ref, wb_ref, sem, *, tm, first_col, stage_cols):
    @pl.when(pl.program_id(0) == 0)
    def _():
        for c in range(0, wb_ref.shape[1], stage_cols):
            piece = pltpu.make_async_copy(w_hbm_ref.at[:, pl.ds(first_col + c, stage_cols)], wf_ref, sem)
            piece.start()
            piece.wait()
            wb_ref[:, c:c + stage_cols] = wf_ref[...].astype(BF16)

    def norm(rows):
        xr = x_ref[rows, :]
        h_ref[rows, :] = (xr * _rms_scale(xr, RMS_EPS) * g_ref[...]).astype(BF16)
    _for_row_chunks(tm, 16, norm)
    o_ref[...] = jnp.dot(h_ref[...], wb_ref[...], preferred_element_type=F32).astype(o_ref.dtype)


def _norm_proj(x, g, w, first_col, n_cols, tm=512, stage_cols=512):
    s, d = x.shape
    return pl.pallas_call(
        functools.partial(_norm_proj_kernel, tm=tm, first_col=first_col, stage_cols=stage_cols),
        grid=(s // tm,),
        in_specs=[pl.BlockSpec((tm, d), lambda i: (i, 0)),
                  pl.BlockSpec((1, d), lambda i: (0, 0)),
                  pl.BlockSpec(memory_space=pl.ANY)],
        out_specs=[pl.BlockSpec((tm, d), lambda i: (i, 0)),
                   pl.BlockSpec((tm, n_cols), lambda i: (i, 0))],
        out_shape=[jax.ShapeDtypeStruct((s, d), BF16), jax.ShapeDtypeStruct((s, n_cols), BF16)],
        scratch_shapes=[pltpu.VMEM((d, stage_cols), F32), pltpu.VMEM((d, n_cols), BF16),
                        pltpu.SemaphoreType.DMA(())],
        compiler_params=_params("arbitrary"),
        name="norm_proj_k",
    )(x, g.reshape(1, d), w)


def _weight_col_block(j, col_blocks):
    block = j + col_blocks[0]
    for pos in range(1, len(col_blocks)):
        gap = col_blocks[pos] - col_blocks[pos - 1] - 1
        if gap:
            block = block + jnp.where(j >= pos, gap, 0)
    return block


def _stage_weight_block(w_hbm_ref, wf_ref, wb_ref, sem, col_blocks, tn):
    j = pl.program_id(0)

    def weight_copy(step):
        col = pl.multiple_of(_weight_col_block(step, col_blocks) * tn, tn)
        return pltpu.make_async_copy(w_hbm_ref.at[:, pl.ds(col, tn)], wf_ref, sem)

    @pl.when(pl.program_id(1) == 0)
    def _():
        @pl.when(j == 0)
        def _():
            weight_copy(0).start()

        weight_copy(j).wait()
        wb_ref[...] = wf_ref[...].astype(BF16)

        @pl.when(j + 1 < pl.num_programs(0))
        def _():
            weight_copy(j + 1).start()


def _weight_stage_scratch(k, tn):
    return [pltpu.VMEM((k, tn), F32), pltpu.VMEM((k, tn), BF16), pltpu.SemaphoreType.DMA(())]


def _matmul_kernel(*refs, col_blocks, tn, n_scaled_blocks, scale, has_residual, transpose_out):
    if has_residual:
        a_ref, w_hbm_ref, x_ref, o_ref, wf_ref, wb_ref, sem = refs
    else:
        a_ref, w_hbm_ref, o_ref, wf_ref, wb_ref, sem = refs
    _stage_weight_block(w_hbm_ref, wf_ref, wb_ref, sem, col_blocks, tn)
    acc = jnp.dot(a_ref[...], wb_ref[...], preferred_element_type=F32)
    if n_scaled_blocks:
        acc = acc * jnp.where(pl.program_id(0) < n_scaled_blocks, scale, 1.0)
    if has_residual:
        acc = x_ref[...] + acc
    if transpose_out:
        acc = acc.T
    o_ref[...] = acc.astype(o_ref.dtype)


def _matmul(a, w, col_blocks, out_dtype, tm, tn, name, n_scaled_blocks=0, scale=1.0, residual=None,
            transpose_out=False):
    s, k = a.shape
    n_cols = len(col_blocks) * tn
    tile = pl.BlockSpec((tm, tn), lambda j, i: (i, j))
    in_specs = [pl.BlockSpec((tm, k), lambda j, i: (i, 0)), pl.BlockSpec(memory_space=pl.ANY)]
    operands = [a, w]
    if residual is not None:
        in_specs.append(tile)
        operands.append(residual)
    if transpose_out:
        out_spec, out_shape = pl.BlockSpec((tn, tm), lambda j, i: (j, i)), (n_cols, s)
    else:
        out_spec, out_shape = tile, (s, n_cols)
    return pl.pallas_call(
        functools.partial(_matmul_kernel, col_blocks=tuple(col_blocks), tn=tn, n_scaled_blocks=n_scaled_blocks,
                          scale=scale, has_residual=residual is not None, transpose_out=transpose_out),
        grid=(len(col_blocks), s // tm),
        in_specs=in_specs,
        out_specs=out_spec,
        out_shape=jax.ShapeDtypeStruct(out_shape, out_dtype),
        scratch_shapes=_weight_stage_scratch(k, tn),
        compiler_params=_params("arbitrary", "arbitrary"),
        name=name,
    )(*operands)


def _attn_kernel(lq1_ref, lk1_ref, lq2_ref, lk2_ref, gsub_ref, qt_ref, qt_next_ref, k_ref, vt_ref, za_ref,
                 side_in_ref, o_ref, side_out_ref, acc1_ref, acc2_ref, sa_ref, sb_ref, ma_ref, mb_ref, *, tq):
    tk = tq // 2
    qi = pl.program_id(1)
    side_out_ref[...] = side_in_ref[...].astype(side_out_ref.dtype)
    lam = (jnp.exp(jnp.sum(lq1_ref[...] * lk1_ref[...], axis=-1, keepdims=True))
           - jnp.exp(jnp.sum(lq2_ref[...] * lk2_ref[...], axis=-1, keepdims=True))
           + LAMBDA_INIT)
    acc_refs = (acc1_ref, acc2_ref)
    acc1_ref[...] = jnp.zeros_like(acc1_ref)
    acc2_ref[...] = jnp.zeros_like(acc2_ref)

    def qk(c, bufs, q0=0, queries_ref=qt_ref, with_max=True):
        s_ref, smax_ref = bufs
        kc = k_ref[pl.ds(pl.multiple_of(c * tk, tk), tk), :]
        for comp in range(2):
            dims = slice(comp * HEAD_DIM, (comp + 1) * HEAD_DIM)
            s = jnp.dot(kc[:, dims], queries_ref[dims, q0:], preferred_element_type=F32)
            s_ref[comp, :, q0:] = s
            if with_max:
                smax_ref[comp, :, q0:] = jnp.max(s, axis=0, keepdims=True)

    def softmax_pv(c, bufs, carry, masked, q0=0):
        s_ref, smax_ref = bufs
        vtc = vt_ref[:, pl.ds(pl.multiple_of(c * tk, tk), tk)]
        new = []
        for comp in range(2):
            m, l = carry[comp]
            s = s_ref[comp, :, q0:]
            if masked:
                key = c * tk + lax.broadcasted_iota(jnp.int32, (tk, tk), 0)
                query = qi * tq + q0 + lax.broadcasted_iota(jnp.int32, (tk, tk), 1)
                causal = jnp.where(key <= query, s[:, :tk], -jnp.inf)
                s = causal if s.shape[1] == tk else jnp.concatenate([causal, s[:, tk:]], axis=1)
                chunk_max = jnp.max(s, axis=0, keepdims=True)
            else:
                chunk_max = smax_ref[comp, :, q0:]
            m_new = jnp.maximum(m[:, q0:], chunk_max)
            alpha = jnp.exp2(m[:, q0:] - m_new)
            p = jnp.exp2(s - m_new)
            l_new = alpha * l[:, q0:] + jnp.sum(p, axis=0, keepdims=True)
            acc_ref = acc_refs[comp]
            acc_ref[:, q0:] = alpha * acc_ref[:, q0:] + jnp.dot(vtc, p.astype(BF16), preferred_element_type=F32)
            if q0:
                m_new = jnp.concatenate([m[:, :q0], m_new], axis=1)
                l_new = jnp.concatenate([l[:, :q0], l_new], axis=1)
            new.append((m_new, l_new))
        return tuple(new)

    buf_a, buf_b = (sa_ref, ma_ref), (sb_ref, mb_ref)

    def pair(j, carry):
        c0 = 2 * j
        qk(c0 + 1, buf_b)
        carry = softmax_pv(c0, buf_a, carry, masked=False)
        qk(c0 + 2, buf_a)
        return softmax_pv(c0 + 1, buf_b, carry, masked=False)

    neg = jnp.full((1, tq), -jnp.inf, F32)
    zero = jnp.zeros((1, tq), F32)

    @pl.when(qi == 0)
    def _():
        qk(0, buf_a, with_max=False)

    carry = lax.fori_loop(0, qi, pair, ((neg, zero), (neg, zero)))
    qk(2 * qi + 1, buf_b, q0=tk, with_max=False)
    carry = softmax_pv(2 * qi, buf_a, carry, masked=True)
    (_, l1), (_, l2) = softmax_pv(2 * qi + 1, buf_b, carry, masked=True, q0=tk)

    qk(0, buf_a, queries_ref=qt_next_ref)

    ot = acc1_ref[...] * (1.0 / l1) - acc2_ref[...] * (lam / l2)
    ot = ot * lax.rsqrt(jnp.mean(ot * ot, axis=0, keepdims=True) + SUBLN_EPS)
    o = ot.T * gsub_ref[...] * (1.0 - LAMBDA_INIT)
    o_ref[...] = (o * _silu(za_ref[...])).astype(o_ref.dtype)


def _attention(qvt, k, rest, lq1, lk1, lq2, lk2, g_subln, side_cast, tq=1024):
    s = k.shape[0]
    n_tiles = s // tq
    side_rows, side_cols = side_cast.shape
    side_spec = pl.BlockSpec((side_rows // (N_HEADS * n_tiles), side_cols), lambda h, i: (h * n_tiles + i, 0))
    vec = pl.BlockSpec((1, HEAD_DIM), lambda h, i: (0, 0))
    score_buf = pltpu.VMEM((2, tq // 2, tq), F32)
    score_max = pltpu.VMEM((2, 1, tq), F32)
    return pl.pallas_call(
        functools.partial(_attn_kernel, tq=tq),
        grid=(N_HEADS, n_tiles),
        in_specs=[vec, vec, vec, vec,
                  pl.BlockSpec((1, V_DIM), lambda h, i: (0, 0)),
                  pl.BlockSpec((V_DIM, tq), lambda h, i: (h, i)),
                  pl.BlockSpec((V_DIM, tq), lambda h, i: (h, jnp.minimum(i + 1, n_tiles - 1))),
                  pl.BlockSpec((s, V_DIM), lambda h, i: (0, h)),
                  pl.BlockSpec((V_DIM, s), lambda h, i: (N_HEADS + h, 0)),
                  pl.BlockSpec((tq, V_DIM), lambda h, i: (i, h)),
                  side_spec],
        out_specs=[pl.BlockSpec((tq, V_DIM), lambda h, i: (i, h)), side_spec],
        out_shape=[jax.ShapeDtypeStruct((s, ATT_WIDTH), BF16), jax.ShapeDtypeStruct(side_cast.shape, BF16)],
        scratch_shapes=[pltpu.VMEM((V_DIM, tq), F32), pltpu.VMEM((V_DIM, tq), F32), score_buf, score_buf,
                        score_max, score_max],
        compiler_params=_params("parallel", "arbitrary"),
        name="diff_attention",
    )(lq1.reshape(1, -1), lk1.reshape(1, -1), lq2.reshape(1, -1), lk2.reshape(1, -1),
      g_subln.reshape(1, -1), qvt, qvt, k, qvt, rest, side_cast)


def _conv_kernel(ga_ref, gb_ref, ga_halo_ref, gb_halo_ref, zc_ref, w_ref, b_ref, lng_ref, lnb_ref,
                 o_ref, ext_ref, y_ref, *, tr, lane_chunk, row_chunk):
    i = pl.program_id(0)
    halo = ga_halo_ref[...] * _sigmoid(gb_halo_ref[...])
    ext_ref[0:CONV_HALO, :] = jnp.where(i > 0, halo, 0.0)

    def glu(rows):
        ext_ref[CONV_HALO + rows.start:CONV_HALO + rows.stop, :] = ga_ref[rows, :] * _sigmoid(gb_ref[rows, :])
    _for_row_chunks(tr, 16, glu)

    first_tap = CONV_HALO - (CONV_KERNEL - 1)
    slab_rows = row_chunk + CONV_HALO

    def lane_body(c, _):
        lanes = pl.ds(pl.multiple_of(c * lane_chunk, lane_chunk), lane_chunk)
        for r0 in range(0, tr, row_chunk):
            slab = ext_ref[r0:r0 + slab_rows, lanes]
            acc = jnp.broadcast_to(b_ref[:, lanes], (row_chunk, lane_chunk))
            for phase in range(8):
                shifted = slab if phase == 0 else pltpu.roll(slab, slab_rows - phase, 0)
                for j in range(CONV_KERNEL):
                    if (first_tap + j) % 8 == phase:
                        off = first_tap + j - phase
                        acc = acc + w_ref[j:j + 1, lanes] * shifted[off:off + row_chunk]
            y_ref[r0:r0 + row_chunk, lanes] = acc
        return 0

    lax.fori_loop(0, CONV_WIDTH // lane_chunk, lane_body, 0)

    def norm_gate(rows):
        y = y_ref[rows, :]
        yc = y - jnp.mean(y, axis=-1, keepdims=True)
        ln = yc * lax.rsqrt(jnp.mean(yc * yc, axis=-1, keepdims=True) + LN_EPS) * lng_ref[...] + lnb_ref[...]
        o_ref[rows, :] = (_silu(ln) * _silu(zc_ref[rows, :])).astype(o_ref.dtype)
    _for_row_chunks(tr, 16, norm_gate)


def _conv_branch(rest, conv_w, conv_b, ln_g, ln_b, tr=512):
    s = rest.shape[0]
    c = CONV_WIDTH
    halo_blocks = tr // CONV_HALO
    row = lambda col: pl.BlockSpec((tr, c), lambda i: (i, col))
    halo = lambda col: pl.BlockSpec((CONV_HALO, c), lambda i: (jnp.maximum(i * halo_blocks - 1, 0), col))
    vec = pl.BlockSpec((1, c), lambda i: (0, 0))
    return pl.pallas_call(
        functools.partial(_conv_kernel, tr=tr, lane_chunk=128, row_chunk=128),
        grid=(s // tr,),
        in_specs=[row(1), row(2), halo(1), halo(2), row(3),
                  pl.BlockSpec((CONV_KERNEL, c), lambda i: (0, 0)), vec, vec, vec],
        out_specs=pl.BlockSpec((tr, c), lambda i: (i, 0)),
        out_shape=jax.ShapeDtypeStruct((s, c), BF16),
        scratch_shapes=[pltpu.VMEM((CONV_HALO + tr, c), F32), pltpu.VMEM((tr, c), F32)],
        compiler_params=_params("parallel"),
        name="conv_branch",
    )(rest, rest, rest, rest, rest, conv_w, conv_b.reshape(1, c), ln_g.reshape(1, c), ln_b.reshape(1, c))


def _merge_kernel(aa_ref, ac_ref, wa_hbm_ref, wc_hbm_ref, ga_ref, gc_ref, o_ref,
                  waf_ref, wab_ref, sem_a, wcf_ref, wcb_ref, sem_c, *, tn):
    col_blocks = tuple(range(D_MODEL // tn))
    _stage_weight_block(wa_hbm_ref, waf_ref, wab_ref, sem_a, col_blocks, tn)
    _stage_weight_block(wc_hbm_ref, wcf_ref, wcb_ref, sem_c, col_blocks, tn)
    ya = jnp.dot(aa_ref[...], wab_ref[...], preferred_element_type=F32)
    yc = jnp.dot(ac_ref[...], wcb_ref[...], preferred_element_type=F32)
    m = _sigmoid(ga_ref[...]) * ya + _sigmoid(gc_ref[...]) * yc
    o_ref[...] = m.astype(o_ref.dtype)


def _merge(a_att, a_conv, w_att_out, w_conv_out, rest, tm=512, tn=1024):
    s, k = a_att.shape
    n = D_MODEL
    gate_a_block = (ATT_WIDTH + 3 * CONV_WIDTH) // tn
    gate_c_block = gate_a_block + D_MODEL // tn
    return pl.pallas_call(
        functools.partial(_merge_kernel, tn=tn),
        grid=(n // tn, s // tm),
        in_specs=[pl.BlockSpec((tm, k), lambda j, i: (i, 0)),
                  pl.BlockSpec((tm, k), lambda j, i: (i, 0)),
                  pl.BlockSpec(memory_space=pl.ANY),
                  pl.BlockSpec(memory_space=pl.ANY),
                  pl.BlockSpec((tm, tn), lambda j, i: (i, j + gate_a_block)),
                  pl.BlockSpec((tm, tn), lambda j, i: (i, j + gate_c_block))],
        out_specs=pl.BlockSpec((tm, tn), lambda j, i: (i, j)),
        out_shape=jax.ShapeDtypeStruct((s, n), BF16),
        scratch_shapes=_weight_stage_scratch(k, tn) + _weight_stage_scratch(k, tn),
        compiler_params=_params("arbitrary", "arbitrary"),
        name="gated_merge",
    )(a_att, a_conv, w_att_out, w_conv_out, rest, rest)


def _ple_kernel(x1_ref, wg_ref, p_ref, wp_ref, ggate_ref, gpost_ref, gfin_ref, o_ref, hg_ref, *, tm, tn):
    def gate_norm(rows):
        xr = x1_ref[rows, :]
        hg_ref[rows, :] = (xr * _rms_scale(xr, RMS_EPS) * ggate_ref[...]).astype(BF16)
    _for_row_chunks(tm, 16, gate_norm)

    o_ref[...] = jnp.dot(p_ref[...].astype(BF16), wp_ref[...], preferred_element_type=F32)

    def embed_norm(rows):
        er = o_ref[rows, :]
        o_ref[rows, :] = er * _rms_scale(er, RMS_EPS) * gpost_ref[...]
    _for_row_chunks(tm, 8, embed_norm)

    for c in range(D_MODEL // tn):
        cols = slice(c * tn, (c + 1) * tn)
        g = _sigmoid(jnp.dot(hg_ref[...], wg_ref[:, cols], preferred_element_type=F32))
        o_ref[:, cols] = x1_ref[:, cols] + g * o_ref[:, cols]

    def final_norm(rows):
        xr = o_ref[rows, :]
        o_ref[rows, :] = xr * _rms_scale(xr, RMS_EPS) * gfin_ref[...]
    _for_row_chunks(tm, 8, final_norm)


def _ple(x1, w_ple_gate, p, w_ple, g_ple_gate, g_ple_post, g_final, tm=256, tn=512):
    s, n = x1.shape
    vec = pl.BlockSpec((1, n), lambda i: (0, 0))
    resident = pl.Buffered(1)
    return pl.pallas_call(
        functools.partial(_ple_kernel, tm=tm, tn=tn),
        grid=(s // tm,),
        in_specs=[pl.BlockSpec((tm, n), lambda i: (i, 0)),
                  pl.BlockSpec((n, n), lambda i: (0, 0), pipeline_mode=resident),
                  pl.BlockSpec((tm, PLE_DIM), lambda i: (i, 0)),
                  pl.BlockSpec((PLE_DIM, n), lambda i: (0, 0), pipeline_mode=resident),
                  vec, vec, vec],
        out_specs=pl.BlockSpec((tm, n), lambda i: (i, 0)),
        out_shape=jax.ShapeDtypeStruct((s, n), F32),
        scratch_shapes=[pltpu.VMEM((tm, n), BF16)],
        compiler_params=_params("parallel"),
        name="ple_final",
    )(x1, w_ple_gate, p, w_ple, g_ple_gate.reshape(1, n), g_ple_post.reshape(1, n), g_final.reshape(1, n))


def kernel(x, p, g_mix, w_in, lambda_q1, lambda_k1, lambda_q2, lambda_k2, g_subln, w_att_out, conv_w,
           conv_b, ln_g, ln_b, w_conv_out, w_out, g_ple_gate, w_ple_gate, w_ple, g_ple_post, g_final):
    b, s, d = x.shape
    assert (b, s, d) == (1, SEQ, D_MODEL) and p.shape[0] == 1 and w_in.shape[0] == 1
    x2d = x.reshape(s, d)

    h, k = _norm_proj(x2d, g_mix[0], w_in[0], first_col=ATT_WIDTH, n_cols=ATT_WIDTH)
    tn = 1024
    att_blocks = ATT_WIDTH // tn
    q_blocks = tuple(range(att_blocks))
    v_blocks = tuple(range(2 * att_blocks, 3 * att_blocks))
    rest_blocks = tuple(range(3 * att_blocks, w_in.shape[-1] // tn))
    qvt = _matmul(h, w_in[0], q_blocks + v_blocks, BF16, tm=1024, tn=tn, name="proj_qv_t",
                  n_scaled_blocks=att_blocks, scale=QK_LOG2_SCALE, transpose_out=True)
    rest = _matmul(h, w_in[0], rest_blocks, F32, tm=1024, tn=tn, name="proj_rest")

    a_att, w_ple_gate_bf16 = _attention(qvt, k, rest, lambda_q1[0], lambda_k1[0], lambda_q2[0], lambda_k2[0],
                                        g_subln[0], side_cast=w_ple_gate[0])
    a_conv = _conv_branch(rest, conv_w[0], conv_b[0], ln_g[0], ln_b[0])
    m = _merge(a_att, a_conv, w_att_out[0], w_conv_out[0], rest)
    x1 = _matmul(m, w_out[0], tuple(range(D_MODEL // tn)), F32, tm=512, tn=tn, name="out_proj", residual=x2d)
    out = _ple(x1, w_ple_gate_bf16, p[0, 0], w_ple[0].astype(BF16),
               g_ple_gate[0], g_ple_post[0], g_final)
    return out.reshape(b, s, d)
```

```python
import functools
import math

import jax
import jax.numpy as jnp
from jax import lax
from jax.experimental import pallas as pl
from jax.experimental.pallas import tpu as pltpu

D_MODEL = 4096
SEQ = 8192
PLE_DIM = 256
N_HEADS = 8
HEAD_DIM = 128
V_DIM = 2 * HEAD_DIM
ATT_WIDTH = N_HEADS * V_DIM
CONV_WIDTH = D_MODEL // 2
CONV_KERNEL = 31

RMS_EPS = 1e-6
SUBLN_EPS = 1e-5
LN_EPS = 1e-5
LAMBDA_INIT = 0.8 - 0.6 * math.exp(-0.3 * 0)
QK_LOG2_SCALE = HEAD_DIM ** -0.5 * math.log2(math.e)

VMEM_LIMIT_BYTES = 56 * 1024 * 1024
CONV_HALO = 32

BF16 = jnp.bfloat16
F32 = jnp.float32


def _params(*semantics):
    return pltpu.CompilerParams(dimension_semantics=semantics, vmem_limit_bytes=VMEM_LIMIT_BYTES)


def _rms_scale(v, eps):
    return lax.rsqrt(jnp.mean(v * v, axis=-1, keepdims=True) + eps)


def _sigmoid(v):
    return 0.5 * jnp.tanh(0.5 * v) + 0.5


def _silu(v):
    half = 0.5 * v
    return half * jnp.tanh(half) + half


def _for_row_chunks(n_rows, chunk, body):
    for r in range(0, n_rows, chunk):
        body(slice(r, r + chunk))


def _norm_proj_kernel(x_ref, g_ref, w_hbm_ref, h_ref, o_ref, wf_ref, wb_ref, sem, *, tm, first_col, stage_cols):
    @pl.when(pl.program_id(0) == 0)
    def _():
        n_pieces = wb_ref.shape[1] // stage_cols

        def piece_copy(c):
            return pltpu.make_async_copy(w_hbm_ref.at[:, pl.ds(first_col + c * stage_cols, stage_cols)],
                                         wf_ref.at[c % 2], sem.at[c % 2])

        piece_copy(0).start()
        for c in range(n_pieces):
            if c + 1 < n_pieces:
                piece_copy(c + 1).start()
            piece_copy(c).wait()
            wb_ref[:, c * stage_cols:(c + 1) * stage_cols] = wf_ref[c % 2].astype(BF16)

    def norm(rows):
        xr = x_ref[rows, :]
        h_ref[rows, :] = (xr * _rms_scale(xr, RMS_EPS) * g_ref[...]).astype(BF16)
    _for_row_chunks(tm, 16, norm)
    o_ref[...] = jnp.dot(h_ref[...], wb_ref[...], preferred_element_type=F32).astype(o_ref.dtype)


def _norm_proj(x, g, w, first_col, n_cols, tm=512, stage_cols=256):
    s, d = x.shape
    return pl.pallas_call(
        functools.partial(_norm_proj_kernel, tm=tm, first_col=first_col, stage_cols=stage_cols),
        grid=(s // tm,),
        in_specs=[pl.BlockSpec((tm, d), lambda i: (i, 0)),
                  pl.BlockSpec((1, d), lambda i: (0, 0)),
                  pl.BlockSpec(memory_space=pl.ANY)],
        out_specs=[pl.BlockSpec((tm, d), lambda i: (i, 0)),
                   pl.BlockSpec((tm, n_cols), lambda i: (i, 0))],
        out_shape=[jax.ShapeDtypeStruct((s, d), BF16), jax.ShapeDtypeStruct((s, n_cols), BF16)],
        scratch_shapes=[pltpu.VMEM((2, d, stage_cols), F32), pltpu.VMEM((d, n_cols), BF16),
                        pltpu.SemaphoreType.DMA((2,))],
        compiler_params=_params("arbitrary"),
        name="norm_proj_k",
    )(x, g.reshape(1, d), w)


def _weight_col_block(j, col_blocks):
    block = j + col_blocks[0]
    for pos in range(1, len(col_blocks)):
        gap = col_blocks[pos] - col_blocks[pos - 1] - 1
        if gap:
            block = block + jnp.where(j >= pos, gap, 0)
    return block


def _stage_weight_block(w_hbm_ref, wf_ref, wb_ref, sem, col_blocks, tn):
    j = pl.program_id(0)

    def weight_copy(step):
        col = pl.multiple_of(_weight_col_block(step, col_blocks) * tn, tn)
        return pltpu.make_async_copy(w_hbm_ref.at[:, pl.ds(col, tn)], wf_ref, sem)

    @pl.when(pl.program_id(1) == 0)
    def _():
        @pl.when(j == 0)
        def _():
            weight_copy(0).start()

        weight_copy(j).wait()
        wb_ref[...] = wf_ref[...].astype(BF16)

        @pl.when(j + 1 < pl.num_programs(0))
        def _():
            weight_copy(j + 1).start()


def _weight_stage_scratch(k, tn):
    return [pltpu.VMEM((k, tn), F32), pltpu.VMEM((k, tn), BF16), pltpu.SemaphoreType.DMA(())]


def _matmul_kernel(*refs, col_blocks, tn, n_scaled_blocks, scale, has_residual, transpose_out):
    if has_residual:
        a_ref, w_hbm_ref, x_ref, o_ref, wf_ref, wb_ref, sem = refs
    else:
        a_ref, w_hbm_ref, o_ref, wf_ref, wb_ref, sem = refs
    _stage_weight_block(w_hbm_ref, wf_ref, wb_ref, sem, col_blocks, tn)
    acc = jnp.dot(a_ref[...], wb_ref[...], preferred_element_type=F32)
    if n_scaled_blocks:
        acc = acc * jnp.where(pl.program_id(0) < n_scaled_blocks, scale, 1.0)
    if has_residual:
        acc = x_ref[...] + acc
    if transpose_out:
        acc = acc.T
    o_ref[...] = acc.astype(o_ref.dtype)


def _matmul(a, w, col_blocks, out_dtype, tm, tn, name, n_scaled_blocks=0, scale=1.0, residual=None,
            transpose_out=False):
    s, k = a.shape
    n_cols = len(col_blocks) * tn
    tile = pl.BlockSpec((tm, tn), lambda j, i: (i, j))
    in_specs = [pl.BlockSpec((tm, k), lambda j, i: (i, 0)), pl.BlockSpec(memory_space=pl.ANY)]
    operands = [a, w]
    if residual is not None:
        in_specs.append(tile)
        operands.append(residual)
    if transpose_out:
        out_spec, out_shape = pl.BlockSpec((tn, tm), lambda j, i: (j, i)), (n_cols, s)
    else:
        out_spec, out_shape = tile, (s, n_cols)
    return pl.pallas_call(
        functools.partial(_matmul_kernel, col_blocks=tuple(col_blocks), tn=tn, n_scaled_blocks=n_scaled_blocks,
                          scale=scale, has_residual=residual is not None, transpose_out=transpose_out),
        grid=(len(col_blocks), s // tm),
        in_specs=in_specs,
        out_specs=out_spec,
        out_shape=jax.ShapeDtypeStruct(out_shape, out_dtype),
        scratch_shapes=_weight_stage_scratch(k, tn),
        compiler_params=_params("arbitrary", "arbitrary"),
        name=name,
    )(*operands)


def _attn_kernel(lq1_ref, lk1_ref, lq2_ref, lk2_ref, gsub_ref, qt_ref, qt_next_ref, k_ref, vt_ref, za_ref,
                 side_in_ref, o_ref, side_out_ref, acc1_ref, acc2_ref, sa_ref, sb_ref, ma_ref, mb_ref, *, tq):
    tk = tq // 2
    qi = pl.program_id(1)
    side_out_ref[...] = side_in_ref[...].astype(side_out_ref.dtype)
    lam = (jnp.exp(jnp.sum(lq1_ref[...] * lk1_ref[...], axis=-1, keepdims=True))
           - jnp.exp(jnp.sum(lq2_ref[...] * lk2_ref[...], axis=-1, keepdims=True))
           + LAMBDA_INIT)
    acc_refs = (acc1_ref, acc2_ref)
    acc1_ref[...] = jnp.zeros_like(acc1_ref)
    acc2_ref[...] = jnp.zeros_like(acc2_ref)

    def qk(c, bufs, q0=0, queries_ref=qt_ref, with_max=True):
        s_ref, smax_ref = bufs
        kc = k_ref[pl.ds(pl.multiple_of(c * tk, tk), tk), :]
        for comp in range(2):
            dims = slice(comp * HEAD_DIM, (comp + 1) * HEAD_DIM)
            s = jnp.dot(kc[:, dims], queries_ref[dims, q0:], preferred_element_type=F32)
            s_ref[comp, :, q0:] = s
            if with_max:
                smax_ref[comp, :, q0:] = jnp.max(s, axis=0, keepdims=True)

    def softmax_pv(c, bufs, carry, masked, q0=0):
        s_ref, smax_ref = bufs
        vtc = vt_ref[:, pl.ds(pl.multiple_of(c * tk, tk), tk)]
        new = []
        for comp in range(2):
            m, l = carry[comp]
            s = s_ref[comp, :, q0:]
            if masked:
                key = c * tk + lax.broadcasted_iota(jnp.int32, (tk, tk), 0)
                query = qi * tq + q0 + lax.broadcasted_iota(jnp.int32, (tk, tk), 1)
                causal = jnp.where(key <= query, s[:, :tk], -jnp.inf)
                s = causal if s.shape[1] == tk else jnp.concatenate([causal, s[:, tk:]], axis=1)
                chunk_max = jnp.max(s, axis=0, keepdims=True)
            else:
                chunk_max = smax_ref[comp, :, q0:]
            m_new = jnp.maximum(m[:, q0:], chunk_max)
            alpha = jnp.exp2(m[:, q0:] - m_new)
            p = jnp.exp2(s - m_new)
            l_new = alpha * l[:, q0:] + jnp.sum(p, axis=0, keepdims=True)
            acc_ref = acc_refs[comp]
            acc_ref[:, q0:] = alpha * acc_ref[:, q0:] + jnp.dot(vtc, p.astype(BF16), preferred_element_type=F32)
            if q0:
                m_new = jnp.concatenate([m[:, :q0], m_new], axis=1)
                l_new = jnp.concatenate([l[:, :q0], l_new], axis=1)
            new.append((m_new, l_new))
        return tuple(new)

    buf_a, buf_b = (sa_ref, ma_ref), (sb_ref, mb_ref)

    def pair(j, carry):
        c0 = 2 * j
        qk(c0 + 1, buf_b)
        carry = softmax_pv(c0, buf_a, carry, masked=False)
        qk(c0 + 2, buf_a)
        return softmax_pv(c0 + 1, buf_b, carry, masked=False)

    neg = jnp.full((1, tq), -jnp.inf, F32)
    zero = jnp.zeros((1, tq), F32)

    @pl.when(qi == 0)
    def _():
        qk(0, buf_a, with_max=False)

    carry = lax.fori_loop(0, qi, pair, ((neg, zero), (neg, zero)))
    qk(2 * qi + 1, buf_b, q0=tk, with_max=False)
    carry = softmax_pv(2 * qi, buf_a, carry, masked=True)
    (_, l1), (_, l2) = softmax_pv(2 * qi + 1, buf_b, carry, masked=True, q0=tk)

    qk(0, buf_a, queries_ref=qt_next_ref)

    ot = acc1_ref[...] * (1.0 / l1) - acc2_ref[...] * (lam / l2)
    ot = ot * lax.rsqrt(jnp.mean(ot * ot, axis=0, keepdims=True) + SUBLN_EPS)
    o = ot.T * gsub_ref[...] * (1.0 - LAMBDA_INIT)
    o_ref[...] = (o * _silu(za_ref[...])).astype(o_ref.dtype)


def _attention(qvt, k, rest, lq1, lk1, lq2, lk2, g_subln, side_cast, tq=1024):
    s = k.shape[0]
    n_tiles = s // tq
    side_rows, side_cols = side_cast.shape
    side_spec = pl.BlockSpec((side_rows // (N_HEADS * n_tiles), side_cols), lambda h, i: (h * n_tiles + i, 0))
    vec = pl.BlockSpec((1, HEAD_DIM), lambda h, i: (0, 0))
    score_buf = pltpu.VMEM((2, tq // 2, tq), F32)
    score_max = pltpu.VMEM((2, 1, tq), F32)
    return pl.pallas_call(
        functools.partial(_attn_kernel, tq=tq),
        grid=(N_HEADS, n_tiles),
        in_specs=[vec, vec, vec, vec,
                  pl.BlockSpec((1, V_DIM), lambda h, i: (0, 0)),
                  pl.BlockSpec((V_DIM, tq), lambda h, i: (h, i)),
                  pl.BlockSpec((V_DIM, tq), lambda h, i: (h, jnp.minimum(i + 1, n_tiles - 1))),
                  pl.BlockSpec((s, V_DIM), lambda h, i: (0, h)),
                  pl.BlockSpec((V_DIM, s), lambda h, i: (N_HEADS + h, 0)),
                  pl.BlockSpec((tq, V_DIM), lambda h, i: (i, h)),
                  side_spec],
        out_specs=[pl.BlockSpec((tq, V_DIM), lambda h, i: (i, h)), side_spec],
        out_shape=[jax.ShapeDtypeStruct((s, ATT_WIDTH), BF16), jax.ShapeDtypeStruct(side_cast.shape, BF16)],
        scratch_shapes=[pltpu.VMEM((V_DIM, tq), F32), pltpu.VMEM((V_DIM, tq), F32), score_buf, score_buf,
                        score_max, score_max],
        compiler_params=_params("parallel", "arbitrary"),
        name="diff_attention",
    )(lq1.reshape(1, -1), lk1.reshape(1, -1), lq2.reshape(1, -1), lk2.reshape(1, -1),
      g_subln.reshape(1, -1), qvt, qvt, k, qvt, rest, side_cast)


def _conv_kernel(ga_ref, gb_ref, ga_halo_ref, gb_halo_ref, zc_ref, w_ref, b_ref, lng_ref, lnb_ref,
                 o_ref, ext_ref, y_ref, *, tr, lane_chunk, row_chunk):
    i = pl.program_id(0)
    halo = ga_halo_ref[...] * _sigmoid(gb_halo_ref[...])
    ext_ref[0:CONV_HALO, :] = jnp.where(i > 0, halo, 0.0)

    def glu(rows):
        ext_ref[CONV_HALO + rows.start:CONV_HALO + rows.stop, :] = ga_ref[rows, :] * _sigmoid(gb_ref[rows, :])
    _for_row_chunks(tr, 16, glu)

    first_tap = CONV_HALO - (CONV_KERNEL - 1)
    slab_rows = row_chunk + CONV_HALO

    def lane_body(c, _):
        lanes = pl.ds(pl.multiple_of(c * lane_chunk, lane_chunk), lane_chunk)
        for r0 in range(0, tr, row_chunk):
            slab = ext_ref[r0:r0 + slab_rows, lanes]
            acc = jnp.broadcast_to(b_ref[:, lanes], (row_chunk, lane_chunk))
            for phase in range(8):
                shifted = slab if phase == 0 else pltpu.roll(slab, slab_rows - phase, 0)
                for j in range(CONV_KERNEL):
                    if (first_tap + j) % 8 == phase:
                        off = first_tap + j - phase
                        acc = acc + w_ref[j:j + 1, lanes] * shifted[off:off + row_chunk]
            y_ref[r0:r0 + row_chunk, lanes] = acc
        return 0

    lax.fori_loop(0, CONV_WIDTH // lane_chunk, lane_body, 0)

    def norm_gate(rows):
        y = y_ref[rows, :]
        yc = y - jnp.mean(y, axis=-1, keepdims=True)
        ln = yc * lax.rsqrt(jnp.mean(yc * yc, axis=-1, keepdims=True) + LN_EPS) * lng_ref[...] + lnb_ref[...]
        o_ref[rows, :] = (_silu(ln) * _silu(zc_ref[rows, :])).astype(o_ref.dtype)
    _for_row_chunks(tr, 16, norm_gate)


def _conv_branch(rest, conv_w, conv_b, ln_g, ln_b, tr=512):
    s = rest.shape[0]
    c = CONV_WIDTH
    halo_blocks = tr // CONV_HALO
    row = lambda col: pl.BlockSpec((tr, c), lambda i: (i, col))
    halo = lambda col: pl.BlockSpec((CONV_HALO, c), lambda i: (jnp.maximum(i * halo_blocks - 1, 0), col))
    vec = pl.BlockSpec((1, c), lambda i: (0, 0))
    return pl.pallas_call(
        functools.partial(_conv_kernel, tr=tr, lane_chunk=128, row_chunk=128),
        grid=(s // tr,),
        in_specs=[row(1), row(2), halo(1), halo(2), row(3),
                  pl.BlockSpec((CONV_KERNEL, c), lambda i: (0, 0)), vec, vec, vec],
        out_specs=pl.BlockSpec((tr, c), lambda i: (i, 0)),
        out_shape=jax.ShapeDtypeStruct((s, c), BF16),
        scratch_shapes=[pltpu.VMEM((CONV_HALO + tr, c), F32), pltpu.VMEM((tr, c), F32)],
        compiler_params=_params("parallel"),
        name="conv_branch",
    )(rest, rest, rest, rest, rest, conv_w, conv_b.reshape(1, c), ln_g.reshape(1, c), ln_b.reshape(1, c))


def _merge_kernel(aa_ref, ac_ref, wa_hbm_ref, wc_hbm_ref, ga_ref, gc_ref, o_ref,
                  waf_ref, wab_ref, sem_a, wcf_ref, wcb_ref, sem_c, *, tn):
    col_blocks = tuple(range(D_MODEL // tn))
    _stage_weight_block(wa_hbm_ref, waf_ref, wab_ref, sem_a, col_blocks, tn)
    _stage_weight_block(wc_hbm_ref, wcf_ref, wcb_ref, sem_c, col_blocks, tn)
    ya = jnp.dot(aa_ref[...], wab_ref[...], preferred_element_type=F32)
    yc = jnp.dot(ac_ref[...], wcb_ref[...], preferred_element_type=F32)
    m = _sigmoid(ga_ref[...]) * ya + _sigmoid(gc_ref[...]) * yc
    o_ref[...] = m.astype(o_ref.dtype)


def _merge(a_att, a_conv, w_att_out, w_conv_out, rest, tm=512, tn=1024):
    s, k = a_att.shape
    n = D_MODEL
    gate_a_block = (ATT_WIDTH + 3 * CONV_WIDTH) // tn
    gate_c_block = gate_a_block + D_MODEL // tn
    return pl.pallas_call(
        functools.partial(_merge_kernel, tn=tn),
        grid=(n // tn, s // tm),
        in_specs=[pl.BlockSpec((tm, k), lambda j, i: (i, 0)),
                  pl.BlockSpec((tm, k), lambda j, i: (i, 0)),
                  pl.BlockSpec(memory_space=pl.ANY),
                  pl.BlockSpec(memory_space=pl.ANY),
                  pl.BlockSpec((tm, tn), lambda j, i: (i, j + gate_a_block)),
                  pl.BlockSpec((tm, tn), lambda j, i: (i, j + gate_c_block))],
        out_specs=pl.BlockSpec((tm, tn), lambda j, i: (i, j)),
        out_shape=jax.ShapeDtypeStruct((s, n), BF16),
        scratch_shapes=_weight_stage_scratch(k, tn) + _weight_stage_scratch(k, tn),
        compiler_params=_params("arbitrary", "arbitrary"),
        name="gated_merge",
    )(a_att, a_conv, w_att_out, w_conv_out, rest, rest)


def _ple_kernel(x1_ref, wg_ref, p_ref, wp_ref, ggate_ref, gpost_ref, gfin_ref, o_ref, hg_ref, *, tm, tn):
    def gate_norm(rows):
        xr = x1_ref[rows, :]
        hg_ref[rows, :] = (xr * _rms_scale(xr, RMS_EPS) * ggate_ref[...]).astype(BF16)
    _for_row_chunks(tm, 16, gate_norm)

    o_ref[...] = jnp.dot(p_ref[...].astype(BF16), wp_ref[...], preferred_element_type=F32)

    def embed_norm(rows):
        er = o_ref[rows, :]
        o_ref[rows, :] = er * _rms_scale(er, RMS_EPS) * gpost_ref[...]
    _for_row_chunks(tm, 8, embed_norm)

    for c in range(D_MODEL // tn):
        cols = slice(c * tn, (c + 1) * tn)
        g = _sigmoid(jnp.dot(hg_ref[...], wg_ref[:, cols], preferred_element_type=F32))
        o_ref[:, cols] = x1_ref[:, cols] + g * o_ref[:, cols]

    def final_norm(rows):
        xr = o_ref[rows, :]
        o_ref[rows, :] = xr * _rms_scale(xr, RMS_EPS) * gfin_ref[...]
    _for_row_chunks(tm, 8, final_norm)


def _ple(x1, w_ple_gate, p, w_ple, g_ple_gate, g_ple_post, g_final, tm=256, tn=512):
    s, n = x1.shape
    vec = pl.BlockSpec((1, n), lambda i: (0, 0))
    resident = pl.Buffered(1)
    return pl.pallas_call(
        functools.partial(_ple_kernel, tm=tm, tn=tn),
        grid=(s // tm,),
        in_specs=[pl.BlockSpec((tm, n), lambda i: (i, 0)),
                  pl.BlockSpec((n, n), lambda i: (0, 0), pipeline_mode=resident),
                  pl.BlockSpec((tm, PLE_DIM), lambda i: (i, 0)),
                  pl.BlockSpec((PLE_DIM, n), lambda i: (0, 0), pipeline_mode=resident),
                  vec, vec, vec],
        out_specs=pl.BlockSpec((tm, n), lambda i: (i, 0)),
        out_shape=jax.ShapeDtypeStruct((s, n), F32),
        scratch_shapes=[pltpu.VMEM((tm, n), BF16)],
        compiler_params=_params("parallel"),
        name="ple_final",
    )(x1, w_ple_gate, p, w_ple, g_ple_gate.reshape(1, n), g_ple_post.reshape(1, n), g_final.reshape(1, n))


def kernel(x, p, g_mix, w_in, lambda_q1, lambda_k1, lambda_q2, lambda_k2, g_subln, w_att_out, conv_w,
           conv_b, ln_g, ln_b, w_conv_out, w_out, g_ple_gate, w_ple_gate, w_ple, g_ple_post, g_final):
    b, s, d = x.shape
    assert (b, s, d) == (1, SEQ, D_MODEL) and p.shape[0] == 1 and w_in.shape[0] == 1
    x2d = x.reshape(s, d)

    h, k = _norm_proj(x2d, g_mix[0], w_in[0], first_col=ATT_WIDTH, n_cols=ATT_WIDTH)
    tn = 1024
    att_blocks = ATT_WIDTH // tn
    q_blocks = tuple(range(att_blocks))
    v_blocks = tuple(range(2 * att_blocks, 3 * att_blocks))
    rest_blocks = tuple(range(3 * att_blocks, w_in.shape[-1] // tn))
    qvt = _matmul(h, w_in[0], q_blocks + v_blocks, BF16, tm=1024, tn=tn, name="proj_qv_t",
                  n_scaled_blocks=att_blocks, scale=QK_LOG2_SCALE, transpose_out=True)
    rest = _matmul(h, w_in[0], rest_blocks, F32, tm=1024, tn=tn, name="proj_rest")

    a_att, w_ple_gate_bf16 = _attention(qvt, k, rest, lambda_q1[0], lambda_k1[0], lambda_q2[0], lambda_k2[0],
                                        g_subln[0], side_cast=w_ple_gate[0])
    a_conv = _conv_branch(rest, conv_w[0], conv_b[0], ln_g[0], ln_b[0])
    m = _merge(a_att, a_conv, w_att_out[0], w_conv_out[0], rest)
    x1 = _matmul(m, w_out[0], tuple(range(D_MODEL // tn)), F32, tm=512, tn=tn, name="out_proj", residual=x2d)
    out = _ple(x1, w_ple_gate_bf16, p[0, 0], w_ple[0].astype(BF16),
               g_ple_gate[0], g_ple_post[0], g_final)
    return out.reshape(b, s, d)
```
